```python
import math
import jax, jax.numpy as jnp
from jax import lax
import numpy as np

D_MODEL = 1024
BATCH = 8
SEQ = 2048
DEPTH = 4

SB_HEAD_DIM = 64
SB_HEADS = D_MODEL // 128
SB_WIDTH = SB_HEADS * SB_HEAD_DIM
SB_BLOCK = 128
RET_QK_DIM = 64
RET_V_DIM = 128
RET_HEADS = D_MODEL // 256
RET_QK_WIDTH = RET_HEADS * RET_QK_DIM
RET_WIDTH = RET_HEADS * RET_V_DIM
RET_CHUNK = 64
ROPE_BASE = 10000.0
GDN_HEAD_DIM = 128
GDN_HEADS = D_MODEL // 256
GDN_WIDTH = GDN_HEADS * GDN_HEAD_DIM
GDN_QKV_WIDTH = 3 * GDN_WIDTH
GDN_CONV = 4
GDN_CHUNK = 64
N_BRANCH = 3
EPS = 1e-6
IN_SIZES = (SB_WIDTH, SB_WIDTH, SB_WIDTH, SB_WIDTH,
            RET_QK_WIDTH, RET_QK_WIDTH, RET_WIDTH, RET_WIDTH,
            GDN_QKV_WIDTH, GDN_WIDTH, GDN_HEADS, GDN_HEADS,
            N_BRANCH * D_MODEL)
IN_WIDTH = sum(IN_SIZES)

kernel_name = 'hybrid_stickbreak_retention_gdn_block'


def _rmsnorm(x, gain):
    xf = x.astype(jnp.float32)
    y = xf * lax.rsqrt(jnp.mean(xf * xf, axis=-1, keepdims=True) + EPS)
    return (y * gain.astype(jnp.float32)).astype(x.dtype)


def _head_layernorm(x):
    mu = jnp.mean(x, axis=-1, keepdims=True)
    xc = x - mu
    return xc * lax.rsqrt(jnp.mean(xc * xc, axis=-1, keepdims=True) + EPS)


def _l2norm(x):
    return x * lax.rsqrt(jnp.sum(x * x, axis=-1, keepdims=True) + EPS)


def _split_heads(x, n_heads):
    b, t, w = x.shape
    return x.reshape(b, t, n_heads, w // n_heads).transpose(0, 2, 1, 3)


def _merge_heads(x):
    b, h, t, d = x.shape
    return x.transpose(0, 2, 1, 3).reshape(b, t, h * d)


def _to_chunks(x, c):
    b, h, t = x.shape[:3]
    x = x.reshape((b, h, t // c, c) + x.shape[3:])
    return jnp.moveaxis(x, 2, 0)


def _from_chunks(x):
    n, b, h, c, d = x.shape
    return jnp.moveaxis(x, 0, 2).reshape(b, h, n * c, d)


def _rotary(x, pos):
    half = x.shape[-1] // 2
    inv_freq = ROPE_BASE ** (-jnp.arange(half, dtype=jnp.float32) / half)
    ang = pos.astype(jnp.float32)[:, None] * inv_freq[None, :]
    cos, sin = jnp.cos(ang), jnp.sin(ang)
    x1, x2 = x[..., :half], x[..., half:]
    return jnp.concatenate([x1 * cos - x2 * sin, x1 * sin + x2 * cos], axis=-1)


def _causal_conv(x, w):
    k, c = w.shape
    return lax.conv_general_dilated(
        x, w[:, None, :].astype(x.dtype), window_strides=(1,), padding=[(k - 1, 0)],
        dimension_numbers=('NWC', 'WIO', 'NWC'), feature_group_count=c)


def _stick_breaking(q, k, v):
    b, h, t, d = q.shape
    nb = t // SB_BLOCK
    qb = jnp.moveaxis((q * d ** -0.5).reshape(b, h, nb, SB_BLOCK, d), 2, 0)
    key_pos = jnp.arange(t)

    def block(args):
        q_blk, blk = args
        z = jnp.einsum('bhqd,bhkd->bhqk', q_blk, k)
        q_pos = blk * SB_BLOCK + jnp.arange(SB_BLOCK)
        valid = key_pos[None, :] < q_pos[:, None]
        log_keep = jnp.where(valid, jax.nn.log_sigmoid(-z), 0.0)
        after = lax.cumsum(log_keep, axis=3, reverse=True) - log_keep
        w = jnp.where(valid, jnp.exp(jax.nn.log_sigmoid(z) + after), 0.0)
        return jnp.einsum('bhqk,bhkd->bhqd', w, v)

    o = lax.map(block, (qb, jnp.arange(nb)))
    return jnp.moveaxis(o, 0, 2).reshape(b, h, t, d)


def _retention(q, k, v):
    c = RET_CHUNK
    b, h, _, dk = q.shape
    dv = v.shape[-1]
    log_gamma = jnp.log(1.0 - 2.0 ** (-5.0 - jnp.arange(h, dtype=jnp.float32)))
    pos = jnp.arange(c, dtype=jnp.float32)
    diff = pos[:, None] - pos[None, :]
    intra_decay = jnp.where(diff >= 0.0,
                            jnp.exp(log_gamma[:, None, None] * jnp.maximum(diff, 0.0)), 0.0)
    q_decay = jnp.exp(log_gamma[:, None] * (pos + 1.0))[:, :, None]
    k_decay = jnp.exp(log_gamma[:, None] * (c - 1.0 - pos))[:, :, None]
    chunk_decay = jnp.exp(log_gamma * c)[:, None, None]
    qc, kc, vc = _to_chunks(q, c), _to_chunks(k, c), _to_chunks(v, c)
    scores = jnp.einsum('nbhid,nbhjd->nbhij', qc, kc) * intra_decay
    intra = jnp.einsum('nbhij,nbhjv->nbhiv', scores, vc)

    def step(r, inp):
        qd_i, kd_i, v_i = inp
        o = jnp.einsum('bhck,bhkv->bhcv', qd_i, r)
        r = r * chunk_decay + jnp.einsum('bhck,bhcv->bhkv', kd_i, v_i)
        return r, o

    r0 = jnp.zeros((b, h, dk, dv), jnp.float32)
    _, inter = lax.scan(step, r0, (qc * q_decay, kc * k_decay, vc))
    return _from_chunks(intra + inter)


def _gated_delta_rule(q, k, v, g, beta):
    c = GDN_CHUNK
    b, h, _, dk = q.shape
    dv = v.shape[-1]
    qc, kc, vc = _to_chunks(q, c), _to_chunks(k, c), _to_chunks(v, c)
    gc = jnp.cumsum(_to_chunks(g, c), axis=-1)
    bc = _to_chunks(beta, c)[..., None]
    idx = jnp.arange(c)
    incl = idx[:, None] >= idx[None, :]
    strict = idx[:, None] > idx[None, :]
    decay = jnp.exp(jnp.where(incl, gc[..., :, None] - gc[..., None, :], -jnp.inf))
    kb = kc * bc
    l_mat = jnp.where(strict, jnp.einsum('nbhid,nbhjd->nbhij', kb, kc) * decay, 0.0)
    rhs = jnp.concatenate([vc * bc, kb * jnp.exp(gc)[..., None]], axis=-1)
    sol = lax.linalg.triangular_solve(l_mat, rhs, left_side=True, lower=True,
                                      unit_diagonal=True)
    u, w = sol[..., :dv], sol[..., dv:]
    scores = jnp.einsum('nbhid,nbhjd->nbhij', qc, kc) * decay
    q_dec = qc * jnp.exp(gc)[..., None]
    k_dec = kc * jnp.exp(gc[..., -1:] - gc)[..., None]
    last = jnp.exp(gc[..., -1])[..., None, None]

    def step(s, inp):
        sc_i, u_i, w_i, qd_i, kd_i, last_i = inp
        v_new = u_i - jnp.einsum('bhck,bhkv->bhcv', w_i, s)
        o = jnp.einsum('bhck,bhkv->bhcv', qd_i, s) + jnp.einsum('bhij,bhjv->bhiv', sc_i, v_new)
        s = s * last_i + jnp.einsum('bhck,bhcv->bhkv', kd_i, v_new)
        return s, o

    s0 = jnp.zeros((b, h, dk, dv), jnp.float32)
    _, o = lax.scan(step, s0, (scores, u, w, q_dec, k_dec, last))
    return _from_chunks(o)


def _hybrid_mixer(hn, w_in, conv_w, a_log, dt_bias, gdn_norm_g, w_sb, w_ret, w_gdn, w_out):
    b, t, _ = hn.shape
    dt = hn.dtype
    f32 = jnp.float32
    proj = jnp.einsum('btd,dn->btn', hn, w_in)
    split_at = np.cumsum(IN_SIZES)[:-1].tolist()
    (sb_q, sb_k, sb_v, sb_z, ret_q, ret_k, ret_v, ret_z,
     gdn_qkv, gdn_z, gdn_b, gdn_a, gate_logits) = jnp.split(proj, split_at, axis=-1)

    o_sb = _stick_breaking(_split_heads(sb_q, SB_HEADS).astype(f32),
                           _split_heads(sb_k, SB_HEADS).astype(f32),
                           _split_heads(sb_v, SB_HEADS).astype(f32))
    y_sb = _merge_heads(o_sb).astype(dt) * jax.nn.silu(sb_z)

    pos = jnp.arange(t)
    rq = _rotary(_split_heads(ret_q, RET_HEADS).astype(f32), pos)
    rk = _rotary(_split_heads(ret_k, RET_HEADS).astype(f32), pos) * RET_QK_DIM ** -0.5
    o_ret = _head_layernorm(_retention(rq, rk, _split_heads(ret_v, RET_HEADS).astype(f32)))
    y_ret = _merge_heads(o_ret).astype(dt) * jax.nn.silu(ret_z)

    qkv = jax.nn.silu(_causal_conv(gdn_qkv, conv_w))
    gq, gk, gv = jnp.split(qkv, 3, axis=-1)
    gq = _l2norm(_split_heads(gq, GDN_HEADS).astype(f32)) * GDN_HEAD_DIM ** -0.5
    gk = _l2norm(_split_heads(gk, GDN_HEADS).astype(f32))
    gv = _split_heads(gv, GDN_HEADS).astype(f32)
    beta = jax.nn.sigmoid(gdn_b.astype(f32)).transpose(0, 2, 1)
    g = (-jnp.exp(a_log.astype(f32))
         * jax.nn.softplus(gdn_a.astype(f32) + dt_bias.astype(f32))).transpose(0, 2, 1)
    o_gdn = _rmsnorm(_gated_delta_rule(gq, gk, gv, g, beta), gdn_norm_g)
    y_gdn = _merge_heads(o_gdn).astype(dt) * jax.nn.silu(gdn_z)

    gates = jax.nn.sigmoid(gate_logits.astype(f32)).astype(dt).reshape(b, t, N_BRANCH, D_MODEL)
    merged = (gates[:, :, 0] * jnp.einsum('btw,wd->btd', y_sb, w_sb)
              + gates[:, :, 1] * jnp.einsum('btw,wd->btd', y_ret, w_ret)
              + gates[:, :, 2] * jnp.einsum('btw,wd->btd', y_gdn, w_gdn))
    return jnp.einsum('btd,de->bte', merged, w_out)


def setup_inputs(seed: int = 0) -> dict:
    key = jax.random.key(seed)
    ks = jax.random.split(key, 12)
    f32 = jnp.float32
    x = jax.random.normal(ks[0], (BATCH, SEQ, D_MODEL), f32)
    norm_g = 1.0 + 0.02 * jax.random.normal(ks[1], (DEPTH, D_MODEL), f32)
    w_in = jax.random.normal(ks[2], (DEPTH, D_MODEL, IN_WIDTH), f32) * D_MODEL ** -0.5
    conv_w = jax.random.normal(ks[3], (DEPTH, GDN_CONV, GDN_QKV_WIDTH), f32) * GDN_CONV ** -0.5
    a_log = jnp.log(jax.random.uniform(ks[4], (DEPTH, GDN_HEADS), f32, 1.0, 16.0))
    dt0 = jnp.exp(jax.random.uniform(ks[5], (DEPTH, GDN_HEADS), f32,
                                     math.log(1e-3), math.log(1e-1)))
    dt_bias = dt0 + jnp.log(-jnp.expm1(-dt0))
    gdn_norm_g = 1.0 + 0.02 * jax.random.normal(ks[6], (DEPTH, GDN_HEAD_DIM), f32)
    w_sb = jax.random.normal(ks[7], (DEPTH, SB_WIDTH, D_MODEL), f32) * SB_WIDTH ** -0.5
    w_ret = jax.random.normal(ks[8], (DEPTH, RET_WIDTH, D_MODEL), f32) * RET_WIDTH ** -0.5
    w_gdn = jax.random.normal(ks[9], (DEPTH, GDN_WIDTH, D_MODEL), f32) * GDN_WIDTH ** -0.5
    w_out = jax.random.normal(ks[10], (DEPTH, D_MODEL, D_MODEL), f32) * D_MODEL ** -0.5
    final_g = 1.0 + 0.02 * jax.random.normal(ks[11], (D_MODEL,), f32)
    return {'x': x, 'norm_g': norm_g, 'w_in': w_in, 'conv_w': conv_w, 'a_log': a_log,
            'dt_bias': dt_bias, 'gdn_norm_g': gdn_norm_g, 'w_sb': w_sb, 'w_ret': w_ret,
            'w_gdn': w_gdn, 'w_out': w_out, 'final_g': final_g}


def reference(x, norm_g, w_in, conv_w, a_log, dt_bias, gdn_norm_g, w_sb, w_ret, w_gdn,
              w_out, final_g):
    for layer in range(DEPTH):
        hn = _rmsnorm(x, norm_g[layer])
        x = x + _hybrid_mixer(hn, w_in[layer], conv_w[layer], a_log[layer], dt_bias[layer],
                              gdn_norm_g[layer], w_sb[layer], w_ret[layer], w_gdn[layer],
                              w_out[layer])
    return _rmsnorm(x, final_g)
```

```python
import functools
import math

import jax
import jax.numpy as jnp
from jax import lax
from jax.experimental import pallas as pl
from jax.experimental.pallas import tpu as pltpu

F32 = jnp.float32
BF16 = jnp.bfloat16

D_MODEL = 1024
EPS = 1e-6
LANES = 128
VMEM_LIMIT = 56 * 1024 * 1024

SB_HEAD_DIM = 64
SB_WIDTH = 512
SB_TILE = 256
RET_QK_DIM = 64
RET_HEADS = 4
RET_QK_WIDTH = 256
RET_WIDTH = 512
RET_CHUNK = 256
ROPE_BASE = 10000.0
GDN_HEADS = 4
GDN_HEAD_DIM = 128
GDN_WIDTH = 512
GDN_CONV = 4
GDN_CHUNK = 128
N_BRANCH = 3

COL_GATE = 0
COL_SB_Q, COL_SB_K, COL_SB_V, COL_SB_Z = 24, 28, 32, 36
COL_RET_Q, COL_RET_K, COL_RET_V, COL_RET_Z = 40, 42, 44, 48
COL_GDN_Q, COL_GDN_K, COL_GDN_V, COL_GDN_Z = 52, 56, 60, 64
PROJ_WIDTH = 68 * LANES
PROJ_TILE_N = 512
_REF_BA = 5632
_REF_GATE = 5640


def _dot(a, b, contract=((1,), (0,)), precision=None):
    return lax.dot_general(a, b, (contract, ((), ())), precision=precision,
                           preferred_element_type=F32)


def _dot_nt(a, b, precision=None):
    return _dot(a, b, ((1,), (1,)), precision)


def _dot_tn(a, b, precision=None):
    return _dot(a, b, ((0,), (0,)), precision)


def _sigmoid(x):
    return 1.0 / (1.0 + jnp.exp(-x))


def _softplus(x):
    return jnp.maximum(x, 0.0) + jnp.log(1.0 + jnp.exp(-jnp.abs(x)))


def _params(*sem):
    return pltpu.CompilerParams(dimension_semantics=sem, vmem_limit_bytes=VMEM_LIMIT)


def _in_proj_kernel(x_ref, g_ref, w_ref, wba_ref, proj_ref, ba_ref, hn_ref):
    @pl.when(pl.program_id(1) == 0)
    def _():
        x = x_ref[...]
        ms = jnp.mean(x * x, axis=-1, keepdims=True)
        hn = (x * lax.rsqrt(ms + EPS) * g_ref[...]).astype(BF16)
        hn_ref[...] = hn
        ba_ref[...] = _dot(hn, wba_ref[...])

    proj_ref[...] = _dot(hn_ref[...], w_ref[...]).astype(BF16)


def _in_proj(x2, gain, w, wba):
    m = x2.shape[0]
    tm = min(2048, m)
    return pl.pallas_call(
        _in_proj_kernel,
        grid=(m // tm, PROJ_WIDTH // PROJ_TILE_N),
        in_specs=[
            pl.BlockSpec((tm, D_MODEL), lambda i, j: (i, 0)),
            pl.BlockSpec((1, D_MODEL), lambda i, j: (0, 0)),
            pl.BlockSpec((D_MODEL, PROJ_TILE_N), lambda i, j: (0, j)),
            pl.BlockSpec((D_MODEL, LANES), lambda i, j: (0, 0)),
        ],
        out_specs=[
            pl.BlockSpec((tm, PROJ_TILE_N), lambda i, j: (i, j)),
            pl.BlockSpec((tm, LANES), lambda i, j: (i, 0)),
        ],
        out_shape=[
            jax.ShapeDtypeStruct((m, PROJ_WIDTH), BF16),
            jax.ShapeDtypeStruct((m, LANES), F32),
        ],
        scratch_shapes=[pltpu.VMEM((tm, D_MODEL), BF16)],
        compiler_params=_params("arbitrary", "arbitrary"),
        name="in_proj",
    )(x2, gain, w, wba)


def _sb_kernel(q_ref, k_ref, v_ref, z_ref, o_ref):
    tq = SB_TILE
    qi = pl.program_id(2)
    lane = lax.broadcasted_iota(jnp.int32, (1, LANES), 1)
    row = lax.broadcasted_iota(jnp.int32, (tq, tq), 0)
    col = lax.broadcasted_iota(jnp.int32, (tq, tq), 1)
    valid = col < row
    tri = jnp.where(row > col, 1.0, 0.0).astype(BF16)
    q = q_ref[...]
    zero = jnp.zeros_like(q)
    qm = (jnp.where(lane < SB_HEAD_DIM, q, zero), jnp.where(lane >= SB_HEAD_DIM, q, zero))

    def tile(h, kb, vb, carry, acc, masked):
        z = _dot_nt(qm[h], kb)
        sp = _softplus(z)
        spm = jnp.where(valid, sp, 0.0) if masked else sp
        hi = spm.astype(BF16)
        lo = (spm - hi.astype(F32)).astype(BF16)
        after = _dot(hi, tri) + _dot(lo, tri)
        w = jnp.exp(z - sp - after - carry)
        if masked:
            w = jnp.where(valid, w, 0.0)
        acc = acc + _dot(w.astype(BF16), vb)
        carry = carry + jnp.sum(spm, axis=-1, keepdims=True)
        return carry, acc

    def block(start, state, masked):
        kb = k_ref[pl.ds(start, tq), :]
        vb = v_ref[pl.ds(start, tq), :]
        c0, a0, c1, a1 = state
        c0, a0 = tile(0, kb, vb, c0, a0, masked)
        c1, a1 = tile(1, kb, vb, c1, a1, masked)
        return c0, a0, c1, a1

    init = (jnp.zeros((tq, 1), F32), jnp.zeros((tq, LANES), F32),
            jnp.zeros((tq, 1), F32), jnp.zeros((tq, LANES), F32))
    state = block(pl.multiple_of(qi * tq, tq), init, True)

    def body(i, state):
        return block(pl.multiple_of((qi - 1 - i) * tq, tq), state, False)

    _, a0, _, a1 = lax.fori_loop(0, qi, body, state)
    o = jnp.where(lane < SB_HEAD_DIM, a0, a1)
    zg = z_ref[...].astype(F32)
    o_ref[...] = (o * (zg * _sigmoid(zg))).astype(BF16)


def _sb_attention(proj, b, t):
    tq = SB_TILE
    nq = t // tq
    return pl.pallas_call(
        _sb_kernel,
        grid=(b, SB_WIDTH // LANES, nq),
        in_specs=[
            pl.BlockSpec((tq, LANES), lambda bi, p, qi: (bi * nq + qi, COL_SB_Q + p)),
            pl.BlockSpec((t, LANES), lambda bi, p, qi: (bi, COL_SB_K + p)),
            pl.BlockSpec((t, LANES), lambda bi, p, qi: (bi, COL_SB_V + p)),
            pl.BlockSpec((tq, LANES), lambda bi, p, qi: (bi * nq + qi, COL_SB_Z + p)),
        ],
        out_specs=pl.BlockSpec((tq, LANES), lambda bi, p, qi: (bi * nq + qi, p)),
        out_shape=jax.ShapeDtypeStruct((b * t, SB_WIDTH), BF16),
        compiler_params=_params("arbitrary", "arbitrary", "arbitrary"),
        name="sb_attn",
    )(proj, proj, proj, proj)


def _ret_kernel(q_ref, k_ref, v_ref, z_ref, cos_ref, sin_ref, o_ref, qr_ref, kr_ref, *, t):
    c = RET_CHUNK
    n_chunks = t // c
    lane = lax.broadcasted_iota(jnp.int32, (1, LANES), 1)
    first_half = (lane % RET_QK_DIM) < (RET_QK_DIM // 2)
    row = lax.broadcasted_iota(jnp.int32, (c, c), 0)
    col = lax.broadcasted_iota(jnp.int32, (c, c), 1)
    diff = (row - col).astype(F32)
    pos = lax.broadcasted_iota(jnp.int32, (c, LANES), 0).astype(F32)

    def rotate(x):
        half = RET_QK_DIM // 2
        other = jnp.where(first_half, pltpu.roll(x, LANES - half, 1), pltpu.roll(x, half, 1))
        return x * cos_ref[...] + other * sin_ref[...]

    for pair in range(RET_HEADS // 2):
        cols = slice(pair * LANES, (pair + 1) * LANES)
        qr_ref[...] = rotate(q_ref[:, cols].astype(F32))
        kr_ref[...] = rotate(k_ref[:, cols].astype(F32))
        for sub in range(2):
            head = 2 * pair + sub
            vcols = slice(head * LANES, (head + 1) * LANES)
            log_gamma = math.log(1.0 - 2.0 ** (-5.0 - head))
            head_lanes = (lane >= RET_QK_DIM) if sub else (lane < RET_QK_DIM)
            intra_decay = jnp.where(diff >= 0.0, jnp.exp(log_gamma * jnp.maximum(diff, 0.0)), 0.0)
            q_decay = jnp.exp(log_gamma * (pos + 1.0))
            k_decay = jnp.exp(log_gamma * (c - 1.0 - pos))
            chunk_decay = math.exp(log_gamma * c)

            def chunk(i, state, head_lanes=head_lanes, intra_decay=intra_decay, q_decay=q_decay,
                      k_decay=k_decay, chunk_decay=chunk_decay, vcols=vcols):
                rows = pl.ds(pl.multiple_of(i * c, c), c)
                qc = jnp.where(head_lanes, qr_ref[rows, :], 0.0)
                kc = jnp.where(head_lanes, kr_ref[rows, :], 0.0)
                vc = v_ref[rows, vcols]
                scores = _dot_nt(qc.astype(BF16), kc.astype(BF16)) * intra_decay
                o = _dot(scores.astype(BF16), vc)
                o = o + _dot((qc * q_decay).astype(BF16), state.astype(BF16))
                state = state * chunk_decay + _dot_tn((kc * k_decay).astype(BF16), vc)
                mu = jnp.mean(o, axis=-1, keepdims=True)
                oc = o - mu
                on = oc * lax.rsqrt(jnp.mean(oc * oc, axis=-1, keepdims=True) + EPS)
                zg = z_ref[rows, vcols].astype(F32)
                o_ref[rows, vcols] = (on * (zg * _sigmoid(zg))).astype(BF16)
                return state

            lax.fori_loop(0, n_chunks, chunk, jnp.zeros((LANES, LANES), F32))


def _retention(proj, cos_t, sin_t, b, t):
    return pl.pallas_call(
        functools.partial(_ret_kernel, t=t),
        grid=(b,),
        in_specs=[
            pl.BlockSpec((t, RET_QK_WIDTH), lambda bi: (bi, COL_RET_Q * LANES // RET_QK_WIDTH)),
            pl.BlockSpec((t, RET_QK_WIDTH), lambda bi: (bi, COL_RET_K * LANES // RET_QK_WIDTH)),
            pl.BlockSpec((t, RET_WIDTH), lambda bi: (bi, COL_RET_V * LANES // RET_WIDTH)),
            pl.BlockSpec((t, RET_WIDTH), lambda bi: (bi, COL_RET_Z * LANES // RET_WIDTH)),
            pl.BlockSpec((t, LANES), lambda bi: (0, 0)),
            pl.BlockSpec((t, LANES), lambda bi: (0, 0)),
        ],
        out_specs=pl.BlockSpec((t, RET_WIDTH), lambda bi: (bi, 0)),
        out_shape=jax.ShapeDtypeStruct((b * t, RET_WIDTH), BF16),
        scratch_shapes=[pltpu.VMEM((t, LANES), F32), pltpu.VMEM((t, LANES), F32)],
        compiler_params=_params("arbitrary"),
        name="retention",
    )(proj, proj, proj, proj, cos_t, sin_t)


def _rope_tables(t):
    half = RET_QK_DIM // 2
    inv_freq = ROPE_BASE ** (-jnp.arange(half, dtype=F32) / half)
    ang = jnp.arange(t).astype(F32)[:, None] * inv_freq[None, :]
    cos, sin = jnp.cos(ang), jnp.sin(ang)
    return jnp.tile(cos, (1, 4)), jnp.tile(jnp.concatenate([-sin, sin], axis=-1), (1, 2))


def _gdn_kernel(q_ref, k_ref, v_ref, z_ref, ba_ref, cwq_ref, cwk_ref, cwv_ref, alog_ref, dtb_ref, gn_ref,
                o_ref, qs_ref, ks_ref, vs_ref, beta_ref, g_ref, *, t):
    c = GDN_CHUNK
    head = pl.program_id(1)
    lane = lax.broadcasted_iota(jnp.int32, (1, LANES), 1)
    tpos = lax.broadcasted_iota(jnp.int32, (t, LANES), 0)
    highest = lax.Precision.HIGHEST

    def conv_silu(x_ref, w_ref):
        x = x_ref[...].astype(F32)
        w = w_ref[...]
        y = x * w[GDN_CONV - 1:GDN_CONV, :]
        for s in range(1, GDN_CONV):
            shifted = jnp.where(tpos >= s, pltpu.roll(x, s, 0), 0.0)
            y = y + shifted * w[GDN_CONV - 1 - s:GDN_CONV - s, :]
        return y * _sigmoid(y)

    def l2norm(x):
        return x * lax.rsqrt(jnp.sum(x * x, axis=-1, keepdims=True) + EPS)

    qs_ref[...] = l2norm(conv_silu(q_ref, cwq_ref)) * GDN_HEAD_DIM ** -0.5
    ks_ref[...] = l2norm(conv_silu(k_ref, cwk_ref))
    vs_ref[...] = conv_silu(v_ref, cwv_ref)
    ba = ba_ref[...]
    beta_all = _sigmoid(ba)
    g_all = -jnp.exp(alog_ref[...]) * _softplus(ba + dtb_ref[...])
    beta_ref[...] = jnp.broadcast_to(
        jnp.sum(jnp.where(lane == head, beta_all, 0.0), axis=-1, keepdims=True), (t, LANES))
    g_ref[...] = jnp.broadcast_to(
        jnp.sum(jnp.where(lane == head + GDN_HEADS, g_all, 0.0), axis=-1, keepdims=True), (t, LANES))

    row = lax.broadcasted_iota(jnp.int32, (c, c), 0)
    col = lax.broadcasted_iota(jnp.int32, (c, c), 1)
    incl = row >= col
    strict = row > col
    tril = jnp.where(incl, 1.0, 0.0)
    eye = jnp.where(row == col, 1.0, 0.0)
    gain = gn_ref[...]

    def chunk(i, state):
        rows = pl.ds(pl.multiple_of(i * c, c), c)
        qc, kc, vc, bb = qs_ref[rows, :], ks_ref[rows, :], vs_ref[rows, :], beta_ref[rows, :]
        gc = _dot(tril, g_ref[rows, :], precision=highest)
        gdiff = gc - gc.T
        decay = jnp.where(incl, jnp.exp(jnp.where(incl, gdiff, 0.0)), 0.0)
        kb = kc * bb
        kcb = kc.astype(BF16)
        l_mat = jnp.where(strict, _dot_nt(kb.astype(BF16), kcb) * decay, 0.0)
        power = -l_mat
        inv = eye + power
        for _ in range(int(math.log2(c)) - 1):
            power = _dot(power, power, precision=highest)
            inv = inv + _dot(inv, power, precision=highest)
        eg = jnp.exp(gc)
        u = _dot(inv, vc * bb, precision=highest)
        w = _dot(inv, kb * eg, precision=highest)
        scores = _dot_nt(qc.astype(BF16), kcb) * decay
        g_last = gc[c - 1:c, :]
        sb = state.astype(BF16)
        v_new = u - _dot(w.astype(BF16), sb)
        vnb = v_new.astype(BF16)
        o = _dot((qc * eg).astype(BF16), sb) + _dot(scores.astype(BF16), vnb)
        state = state * jnp.exp(g_last) + _dot_tn((kc * jnp.exp(g_last - gc)).astype(BF16), vnb)
        on = o * lax.rsqrt(jnp.mean(o * o, axis=-1, keepdims=True) + EPS) * gain
        zg = z_ref[rows, :].astype(F32)
        o_ref[rows, :] = (on * (zg * _sigmoid(zg))).astype(BF16)
        return state

    lax.fori_loop(0, t // c, chunk, jnp.zeros((LANES, LANES), F32))


def _gdn(proj, ba, conv_w, alog_row, dtb_row, gn_row, b, t):
    col_spec = lambda base: pl.BlockSpec((t, LANES), lambda bi, h: (bi, base + h))
    conv_spec = lambda base: pl.BlockSpec((GDN_CONV, LANES), lambda bi, h: (0, base + h))
    row_spec = pl.BlockSpec((1, LANES), lambda bi, h: (0, 0))
    return pl.pallas_call(
        functools.partial(_gdn_kernel, t=t),
        grid=(b, GDN_HEADS),
        in_specs=[
            col_spec(COL_GDN_Q), col_spec(COL_GDN_K), col_spec(COL_GDN_V), col_spec(COL_GDN_Z),
            pl.BlockSpec((t, LANES), lambda bi, h: (bi, 0)),
            conv_spec(0), conv_spec(GDN_HEADS), conv_spec(2 * GDN_HEADS),
            row_spec, row_spec, row_spec,
        ],
        out_specs=pl.BlockSpec((t, LANES), lambda bi, h: (bi, h)),
        out_shape=jax.ShapeDtypeStruct((b * t, GDN_WIDTH), BF16),
        scratch_shapes=[pltpu.VMEM((t, LANES), F32) for _ in range(5)],
        compiler_params=_params("arbitrary", "arbitrary"),
        name="gdn",
    )(proj, proj, proj, proj, ba, conv_w, conv_w, conv_w, alog_row, dtb_row, gn_row)


def _merge_kernel(x_ref, g0_ref, g1_ref, g2_ref, ysb_ref, yret_ref, ygdn_ref, wsb_ref, wret_ref, wgdn_ref,
                  wout_ref, fg_ref, o_ref, *, final):
    def branch(g_ref, y_ref, w_ref):
        return _sigmoid(g_ref[...].astype(F32)) * _dot(y_ref[...], w_ref[...])

    merged = (branch(g0_ref, ysb_ref, wsb_ref) + branch(g1_ref, yret_ref, wret_ref)
              + branch(g2_ref, ygdn_ref, wgdn_ref))
    x = x_ref[...] + _dot(merged.astype(BF16), wout_ref[...])
    if final:
        x = x * lax.rsqrt(jnp.mean(x * x, axis=-1, keepdims=True) + EPS) * fg_ref[...]
    o_ref[...] = x


def _merge_out(x2, proj, ysb, yret, ygdn, wsb, wret, wgdn, wout, final_g, final):
    m = x2.shape[0]
    tm = min(512, m)
    tile = lambda width, j=0: pl.BlockSpec((tm, width), lambda i: (i, j))
    whole = lambda a: pl.BlockSpec(a.shape, lambda i: (0, 0))
    return pl.pallas_call(
        functools.partial(_merge_kernel, final=final),
        grid=(m // tm,),
        in_specs=[
            tile(D_MODEL), tile(D_MODEL, 0), tile(D_MODEL, 1), tile(D_MODEL, 2),
            tile(SB_WIDTH), tile(RET_WIDTH), tile(GDN_WIDTH),
            whole(wsb), whole(wret), whole(wgdn), whole(wout), whole(final_g),
        ],
        out_specs=tile(D_MODEL),
        out_shape=jax.ShapeDtypeStruct((m, D_MODEL), F32),
        compiler_params=_params("arbitrary"),
        name="merge_out",
    )(x2, proj, proj, proj, ysb, yret, ygdn, wsb, wret, wgdn, wout, final_g)


def _prep_in_weights(w_in):
    sizes = (SB_WIDTH, SB_WIDTH, SB_WIDTH, SB_WIDTH, RET_QK_WIDTH, RET_QK_WIDTH, RET_WIDTH, RET_WIDTH,
             3 * GDN_WIDTH, GDN_WIDTH)
    scales = [1.0] * len(sizes)
    scales[0] = SB_HEAD_DIM ** -0.5
    scales[5] = RET_QK_DIM ** -0.5
    parts, start = [w_in[..., _REF_GATE:]], 0
    for size, scale in zip(sizes, scales):
        piece = w_in[..., start:start + size]
        parts.append(piece if scale == 1.0 else piece * scale)
        start += size
    w_main = jnp.concatenate(parts, axis=-1).astype(BF16)
    w_ba = w_in[..., _REF_BA:_REF_BA + 2 * GDN_HEADS]
    w_ba = jnp.pad(w_ba, ((0, 0), (0, 0), (0, LANES - 2 * GDN_HEADS))).astype(BF16)
    return w_main, w_ba


def kernel(x, norm_g, w_in, conv_w, a_log, dt_bias, gdn_norm_g, w_sb, w_ret, w_gdn, w_out, final_g):
    b, t, d = x.shape
    depth = w_in.shape[0]
    assert d == D_MODEL and t % SB_TILE == 0 and t % RET_CHUNK == 0 and t % GDN_CHUNK == 0
    w_main, w_ba = _prep_in_weights(w_in)
    w_sb, w_ret, w_gdn, w_out = (w.astype(BF16) for w in (w_sb, w_ret, w_gdn, w_out))
    pad_heads = lambda v: jnp.pad(v, ((0, 0), (GDN_HEADS, LANES - 2 * GDN_HEADS)))[:, None, :]
    alog_rows, dtb_rows = pad_heads(a_log.astype(F32)), pad_heads(dt_bias.astype(F32))
    cos_t, sin_t = _rope_tables(t)
    final_row = final_g.astype(F32)[None, :]

    x2 = x.reshape(b * t, d)
    for layer in range(depth):
        proj, ba = _in_proj(x2, norm_g[layer][None, :], w_main[layer], w_ba[layer])
        ysb = _sb_attention(proj, b, t)
        yret = _retention(proj, cos_t, sin_t, b, t)
        ygdn = _gdn(proj, ba, conv_w[layer], alog_rows[layer], dtb_rows[layer], gdn_norm_g[layer][None, :], b, t)
        x2 = _merge_out(x2, proj, ysb, yret, ygdn, w_sb[layer], w_ret[layer], w_gdn[layer], w_out[layer],
                        final_row, final=(layer == depth - 1))
    return x2.reshape(b, t, d)
```

```python
import functools
import math

import jax
import jax.numpy as jnp
from jax import lax
from jax.experimental import pallas as pl
from jax.experimental.pallas import tpu as pltpu

F32 = jnp.float32
BF16 = jnp.bfloat16

D_MODEL = 1024
EPS = 1e-6
LOG2E = math.log2(math.e)
LANES = 128
SUBLANES = 8
VMEM_LIMIT = 56 * 1024 * 1024

SB_HEAD_DIM = 64
SB_WIDTH = 512
SB_TILE = 256
RET_QK_DIM = 64
RET_HEADS = 4
RET_QK_WIDTH = 256
RET_WIDTH = 512
RET_CHUNK = 256
ROPE_BASE = 10000.0
GDN_HEADS = 4
GDN_HEAD_DIM = 128
GDN_WIDTH = 512
GDN_CONV = 4
GDN_CHUNK = 128
GDN_INV_BASE = 16
GDN_PREP_GROUP = 8
N_BRANCH = 3

COL_GATE = 0
COL_SB_Q, COL_SB_K, COL_SB_V, COL_SB_Z = 24, 28, 32, 36
COL_RET_Q, COL_RET_K, COL_RET_V, COL_RET_Z = 40, 42, 44, 48
COL_GDN_Q, COL_GDN_K, COL_GDN_V, COL_GDN_Z = 52, 56, 60, 64
PROJ_WIDTH = 68 * LANES
PROJ_TILE_N = 512
_REF_BA = 5632
_REF_GATE = 5640


def _dot(a, b, contract=((1,), (0,)), precision=None):
    return lax.dot_general(a, b, (contract, ((), ())), precision=precision,
                           preferred_element_type=F32)


def _dot_nt(a, b, precision=None):
    return _dot(a, b, ((1,), (1,)), precision)


def _dot_tn(a, b, precision=None):
    return _dot(a, b, ((0,), (0,)), precision)


def _split_bf16(a):
    hi = a.astype(BF16)
    return hi, (a - hi.astype(F32)).astype(BF16)


def _dot_inv(a, b):
    return _dot(a.astype(BF16), b.astype(BF16))


def _sigmoid(x):
    return 1.0 / (1.0 + jnp.exp(-x))


def _softplus(x):
    return jnp.maximum(x, 0.0) + jnp.log(1.0 + jnp.exp(-jnp.abs(x)))


def _params(*sem):
    return pltpu.CompilerParams(dimension_semantics=sem, vmem_limit_bytes=VMEM_LIMIT)


def _in_proj_kernel(x_ref, g_ref, w_ref, wba_ref, proj_ref, ba_ref, hn_ref):
    @pl.when(pl.program_id(1) == 0)
    def _():
        x = x_ref[...]
        ms = jnp.mean(x * x, axis=-1, keepdims=True)
        hn = (x * lax.rsqrt(ms + EPS) * g_ref[...]).astype(BF16)
        hn_ref[...] = hn
        ba_ref[...] = _dot(hn, wba_ref[...])

    proj_ref[...] = _dot(hn_ref[...], w_ref[...]).astype(BF16)


def _in_proj(x2, gain, w, wba):
    m = x2.shape[0]
    tm = min(2048, m)
    return pl.pallas_call(
        _in_proj_kernel,
        grid=(m // tm, PROJ_WIDTH // PROJ_TILE_N),
        in_specs=[
            pl.BlockSpec((tm, D_MODEL), lambda i, j: (i, 0)),
            pl.BlockSpec((1, D_MODEL), lambda i, j: (0, 0)),
            pl.BlockSpec((D_MODEL, PROJ_TILE_N), lambda i, j: (0, j)),
            pl.BlockSpec((D_MODEL, LANES), lambda i, j: (0, 0)),
        ],
        out_specs=[
            pl.BlockSpec((tm, PROJ_TILE_N), lambda i, j: (i, j)),
            pl.BlockSpec((tm, LANES), lambda i, j: (i, 0)),
        ],
        out_shape=[
            jax.ShapeDtypeStruct((m, PROJ_WIDTH), BF16),
            jax.ShapeDtypeStruct((m, LANES), F32),
        ],
        scratch_shapes=[pltpu.VMEM((tm, D_MODEL), BF16)],
        compiler_params=_params("arbitrary", "arbitrary"),
        name="in_proj",
    )(x2, gain, w, wba)


def _sb_kernel(q_ref, k_ref, v_ref, z_ref, o_ref):
    tq = SB_TILE
    n_pairs = SB_WIDTH // LANES
    qi = pl.program_id(1)
    lane = lax.broadcasted_iota(jnp.int32, (1, LANES), 1)
    row = lax.broadcasted_iota(jnp.int32, (tq, tq), 0)
    col = lax.broadcasted_iota(jnp.int32, (tq, tq), 1)
    tri = jnp.where(row > col, 1.0, 0.0).astype(BF16)
    row2 = lax.broadcasted_iota(jnp.int32, (2 * tq, tq), 0) % tq
    col2 = lax.broadcasted_iota(jnp.int32, (2 * tq, tq), 1)
    valid = col2 < row2
    lanes_of = lambda p: slice(p * LANES, (p + 1) * LANES)

    q = q_ref[...]
    zero = jnp.zeros((tq, LANES), BF16)
    q_pairs = []
    for p in range(n_pairs):
        qp = q[:, lanes_of(p)]
        q_pairs.append(jnp.concatenate(
            [jnp.where(lane < SB_HEAD_DIM, qp, zero), jnp.where(lane >= SB_HEAD_DIM, qp, zero)], axis=0))

    def block(start, state, masked):
        kb = k_ref[pl.ds(start, tq), :]
        vb = v_ref[pl.ds(start, tq), :]
        carry, acc = state
        log_sig, sp_bf16, rowsum, w, new_acc = {}, {}, {}, {}, {}

        def logits(p):
            z = _dot_nt(q_pairs[p], kb[:, lanes_of(p)])
            log_sig[p] = jnp.minimum(z, 0.0) - jnp.log2(1.0 + jnp.exp2(-jnp.abs(z)))
            sp = z - log_sig[p]
            if masked:
                sp = jnp.where(valid, sp, 0.0)
            sp_bf16[p] = sp.astype(BF16)
            rowsum[p] = jnp.sum(sp, axis=-1, keepdims=True)

        def weights(p):
            after = _dot(sp_bf16[p], tri)
            w[p] = jnp.exp2(log_sig[p] - after - carry[p])
            if masked:
                w[p] = jnp.where(valid, w[p], 0.0)

        def values(p):
            new_acc[p] = acc[p] + _dot(w[p].astype(BF16), vb[:, lanes_of(p)])

        stages = (logits, weights, values)
        for step in range(n_pairs + len(stages) - 1):
            for depth, stage in enumerate(stages):
                if 0 <= step - depth < n_pairs:
                    stage(step - depth)
        return ([carry[p] + rowsum[p] for p in range(n_pairs)], [new_acc[p] for p in range(n_pairs)])

    init = ([jnp.zeros((2 * tq, 1), F32)] * n_pairs, [jnp.zeros((2 * tq, LANES), F32)] * n_pairs)
    state = block(pl.multiple_of(qi * tq, tq), init, True)

    def body(i, state):
        return block(pl.multiple_of((qi - 1 - i) * tq, tq), state, False)

    _, acc = lax.fori_loop(0, qi, body, state)
    o = jnp.concatenate([jnp.where(lane < SB_HEAD_DIM, acc[p][:tq, :], acc[p][tq:, :]) for p in range(n_pairs)],
                        axis=1)
    zg = z_ref[...].astype(F32)
    o_ref[...] = (o * (zg * _sigmoid(zg))).astype(BF16)


def _sb_attention(proj, b, t):
    tq = SB_TILE
    nq = t // tq
    return pl.pallas_call(
        _sb_kernel,
        grid=(b, nq),
        in_specs=[
            pl.BlockSpec((tq, SB_WIDTH), lambda bi, qi: (bi * nq + qi, COL_SB_Q * LANES // SB_WIDTH)),
            pl.BlockSpec((t, SB_WIDTH), lambda bi, qi: (bi, COL_SB_K * LANES // SB_WIDTH)),
            pl.BlockSpec((t, SB_WIDTH), lambda bi, qi: (bi, COL_SB_V * LANES // SB_WIDTH)),
            pl.BlockSpec((tq, SB_WIDTH), lambda bi, qi: (bi * nq + qi, COL_SB_Z * LANES // SB_WIDTH)),
        ],
        out_specs=pl.BlockSpec((tq, SB_WIDTH), lambda bi, qi: (bi * nq + qi, 0)),
        out_shape=jax.ShapeDtypeStruct((b * t, SB_WIDTH), BF16),
        compiler_params=_params("arbitrary", "arbitrary"),
        name="sb_attn",
    )(proj, proj, proj, proj)


def _ret_kernel(q_ref, k_ref, v_ref, z_ref, cos_ref, sin_ref, o_ref, qr_ref, kr_ref, *, t):
    c = RET_CHUNK
    n_chunks = t // c
    lane = lax.broadcasted_iota(jnp.int32, (1, LANES), 1)
    first_half = (lane % RET_QK_DIM) < (RET_QK_DIM // 2)
    row = lax.broadcasted_iota(jnp.int32, (c, c), 0)
    col = lax.broadcasted_iota(jnp.int32, (c, c), 1)
    diff = (row - col).astype(F32)
    pos = lax.broadcasted_iota(jnp.int32, (c, LANES), 0).astype(F32)

    def rotate(x):
        half = RET_QK_DIM // 2
        other = jnp.where(first_half, pltpu.roll(x, LANES - half, 1), pltpu.roll(x, half, 1))
        return x * cos_ref[...] + other * sin_ref[...]

    for pair in range(RET_HEADS // 2):
        cols = slice(pair * LANES, (pair + 1) * LANES)
        qr_ref[...] = rotate(q_ref[:, cols].astype(F32))
        kr_ref[...] = rotate(k_ref[:, cols].astype(F32))
        for sub in range(2):
            head = 2 * pair + sub
            vcols = slice(head * LANES, (head + 1) * LANES)
            log_gamma = math.log(1.0 - 2.0 ** (-5.0 - head))
            head_lanes = (lane >= RET_QK_DIM) if sub else (lane < RET_QK_DIM)
            intra_decay = jnp.where(diff >= 0.0, jnp.exp(log_gamma * jnp.maximum(diff, 0.0)), 0.0)
            q_decay = jnp.exp(log_gamma * (pos + 1.0))
            k_decay = jnp.exp(log_gamma * (c - 1.0 - pos))
            chunk_decay = math.exp(log_gamma * c)

            def chunk(i, state, head_lanes=head_lanes, intra_decay=intra_decay, q_decay=q_decay,
                      k_decay=k_decay, chunk_decay=chunk_decay, vcols=vcols):
                rows = pl.ds(pl.multiple_of(i * c, c), c)
                qc = jnp.where(head_lanes, qr_ref[rows, :], 0.0)
                kc = jnp.where(head_lanes, kr_ref[rows, :], 0.0)
                vc = v_ref[rows, vcols]
                scores = _dot_nt(qc.astype(BF16), kc.astype(BF16)) * intra_decay
                o = _dot(scores.astype(BF16), vc)
                o = o + _dot((qc * q_decay).astype(BF16), state.astype(BF16))
                state = state * chunk_decay + _dot_tn((kc * k_decay).astype(BF16), vc)
                mu = jnp.mean(o, axis=-1, keepdims=True)
                oc = o - mu
                on = oc * lax.rsqrt(jnp.mean(oc * oc, axis=-1, keepdims=True) + EPS)
                zg = z_ref[rows, vcols].astype(F32)
                o_ref[rows, vcols] = (on * (zg * _sigmoid(zg))).astype(BF16)
                return state

            lax.fori_loop(0, n_chunks, chunk, jnp.zeros((LANES, LANES), F32))


def _retention(proj, cos_t, sin_t, b, t):
    return pl.pallas_call(
        functools.partial(_ret_kernel, t=t),
        grid=(b,),
        in_specs=[
            pl.BlockSpec((t, RET_QK_WIDTH), lambda bi: (bi, COL_RET_Q * LANES // RET_QK_WIDTH)),
            pl.BlockSpec((t, RET_QK_WIDTH), lambda bi: (bi, COL_RET_K * LANES // RET_QK_WIDTH)),
            pl.BlockSpec((t, RET_WIDTH), lambda bi: (bi, COL_RET_V * LANES // RET_WIDTH)),
            pl.BlockSpec((t, RET_WIDTH), lambda bi: (bi, COL_RET_Z * LANES // RET_WIDTH)),
            pl.BlockSpec((t, LANES), lambda bi: (0, 0)),
            pl.BlockSpec((t, LANES), lambda bi: (0, 0)),
        ],
        out_specs=pl.BlockSpec((t, RET_WIDTH), lambda bi: (bi, 0)),
        out_shape=jax.ShapeDtypeStruct((b * t, RET_WIDTH), BF16),
        scratch_shapes=[pltpu.VMEM((t, LANES), F32), pltpu.VMEM((t, LANES), F32)],
        compiler_params=_params("arbitrary"),
        name="retention",
    )(proj, proj, proj, proj, cos_t, sin_t)


def _rope_tables(t):
    half = RET_QK_DIM // 2
    inv_freq = ROPE_BASE ** (-jnp.arange(half, dtype=F32) / half)
    ang = jnp.arange(t).astype(F32)[:, None] * inv_freq[None, :]
    cos, sin = jnp.cos(ang), jnp.sin(ang)
    return jnp.tile(cos, (1, 4)), jnp.tile(jnp.concatenate([-sin, sin], axis=-1), (1, 2))


def _gdn_kernel(q_ref, k_ref, v_ref, z_ref, ba_ref, cwq_ref, cwk_ref, cwv_ref, alog_ref, dtb_ref, gn_ref,
                o_ref, qs_ref, ks_ref, vs_ref, beta_ref, g_ref, u_ref, elast_ref, w_ref, sc_ref, qd_ref, kd_ref,
                *, t):
    c = GDN_CHUNK
    lane = lax.broadcasted_iota(jnp.int32, (1, LANES), 1)
    tpos = lax.broadcasted_iota(jnp.int32, (t, LANES), 0)
    lanes_of = lambda h: slice(h * LANES, (h + 1) * LANES)

    def conv_silu(x_ref, w_ref, h):
        x = x_ref[:, lanes_of(h)].astype(F32)
        w = w_ref[:, lanes_of(h)]
        y = x * w[GDN_CONV - 1:GDN_CONV, :]
        for s in range(1, GDN_CONV):
            shifted = jnp.where(tpos >= s, pltpu.roll(x, s, 0), 0.0)
            y = y + shifted * w[GDN_CONV - 1 - s:GDN_CONV - s, :]
        return y * _sigmoid(y)

    def l2norm(x):
        return x * lax.rsqrt(jnp.sum(x * x, axis=-1, keepdims=True) + EPS)

    row = lax.broadcasted_iota(jnp.int32, (c, c), 0)
    col = lax.broadcasted_iota(jnp.int32, (c, c), 1)
    incl = row >= col
    strict = row > col
    tril = jnp.where(incl, 1.0, 0.0).astype(BF16)
    base_blocks = (row // GDN_INV_BASE) == (col // GDN_INV_BASE)
    sibling_blocks = []
    size = GDN_INV_BASE
    while size < c:
        sibling_blocks.append(((row // size) % 2 == 1) & ((col // size) == (row // size) - 1))
        size *= 2
    gain = gn_ref[...]
    prep_in = (qs_ref, ks_ref, vs_ref, beta_ref, g_ref)

    def each(fn, *lists):
        return [fn(*args) for args in zip(*lists)]

    def prepare_group(i, head):
        first = i * GDN_PREP_GROUP
        rows = [pl.ds(pl.multiple_of((first + j) * c, c), c) for j in range(GDN_PREP_GROUP)]
        qc, kc, vc, bb, g = ([ref[r, :] for r in rows] for ref in prep_in)
        g_split = each(_split_bf16, g)
        gc = each(lambda s: _dot(tril, s[0]) + _dot(tril, s[1]), g_split)
        decay = each(lambda x: jnp.where(incl, jnp.exp(jnp.where(incl, x - x.T, 0.0)), 0.0), gc)
        kb = each(lambda k, b_: k * b_, kc, bb)
        kcb = each(lambda k: k.astype(BF16), kc)
        neg_l = each(lambda k_b, k_c, d: -jnp.where(strict, _dot_nt(k_b.astype(BF16), k_c) * d, 0.0), kb, kcb, decay)
        power = each(lambda n: jnp.where(base_blocks, n, 0.0), neg_l)
        corr = power
        for _ in range(int(math.log2(GDN_INV_BASE)) - 1):
            power = each(lambda p: _dot_inv(p, p), power)
            corr = each(lambda m, p: m + p + _dot_inv(m, p), corr, power)
        for sib in sibling_blocks:
            c_off = each(lambda n: jnp.where(sib, n, 0.0), neg_l)
            left = each(lambda m, x: x + _dot_inv(m, x), corr, c_off)
            corr = each(lambda m, x: m + x + _dot_inv(x, m), corr, left)
        eg = each(jnp.exp, gc)
        vb = each(lambda v, b_: v * b_, vc, bb)
        kbe = each(lambda k_b, e: k_b * e, kb, eg)
        outs = (
            (u_ref, each(lambda m, x: x + _dot_inv(m, x), corr, vb)),
            (w_ref, each(lambda m, x: (x + _dot_inv(m, x)).astype(BF16), corr, kbe)),
            (sc_ref, each(lambda q, k_c, d: (_dot_nt(q.astype(BF16), k_c) * d).astype(BF16), qc, kcb, decay)),
            (qd_ref, each(lambda q, e: (q * e).astype(BF16), qc, eg)),
            (kd_ref, each(lambda k, x: (k * jnp.exp(x[c - 1:c, :] - x)).astype(BF16), kc, gc)),
        )
        for ref, vals in outs:
            for r, val in zip(rows, vals):
                ref[head, r, :] = val
        for j, x in enumerate(gc):
            elast_ref[head, pl.ds(pl.multiple_of((first + j) * SUBLANES, SUBLANES), SUBLANES), :] = (
                jnp.broadcast_to(jnp.exp(x[c - 1:c, :]), (SUBLANES, LANES)))
        return head

    ba = ba_ref[...]
    beta_all = _sigmoid(ba)
    g_all = -jnp.exp(alog_ref[...]) * _softplus(ba + dtb_ref[...])
    for head in range(GDN_HEADS):
        qs_ref[...] = l2norm(conv_silu(q_ref, cwq_ref, head)) * GDN_HEAD_DIM ** -0.5
        ks_ref[...] = l2norm(conv_silu(k_ref, cwk_ref, head))
        vs_ref[...] = conv_silu(v_ref, cwv_ref, head)
        beta_ref[...] = jnp.broadcast_to(
            jnp.sum(jnp.where(lane == head, beta_all, 0.0), axis=-1, keepdims=True), (t, LANES))
        g_ref[...] = jnp.broadcast_to(
            jnp.sum(jnp.where(lane == head + GDN_HEADS, g_all, 0.0), axis=-1, keepdims=True), (t, LANES))
        lax.fori_loop(0, t // (c * GDN_PREP_GROUP), prepare_group, head)

    heads = list(range(GDN_HEADS))

    def scan(i, states):
        rows = pl.ds(pl.multiple_of(i * c, c), c)
        last_rows = pl.ds(pl.multiple_of(i * SUBLANES, SUBLANES), SUBLANES)
        sb = each(lambda s: s.astype(BF16), states)
        vnb = each(lambda h, s: (u_ref[h, rows, :] - _dot(w_ref[h, rows, :], s)).astype(BF16), heads, sb)
        o = each(lambda h, s, v: _dot(qd_ref[h, rows, :], s) + _dot(sc_ref[h, rows, :], v), heads, sb, vnb)
        states = each(lambda h, s, v: s * elast_ref[h, last_rows, :][:1, :] + _dot_tn(kd_ref[h, rows, :], v),
                      heads, states, vnb)
        for h, x in zip(heads, o):
            on = x * lax.rsqrt(jnp.mean(x * x, axis=-1, keepdims=True) + EPS) * gain
            zg = z_ref[rows, lanes_of(h)].astype(F32)
            o_ref[rows, lanes_of(h)] = (on * (zg * _sigmoid(zg))).astype(BF16)
        return states

    lax.fori_loop(0, t // c, scan, [jnp.zeros((LANES, LANES), F32)] * GDN_HEADS)


def _gdn(proj, ba, conv_w, alog_row, dtb_row, gn_row, b, t):
    wide = lambda base: pl.BlockSpec((t, GDN_WIDTH), lambda bi: (bi, base * LANES // GDN_WIDTH))
    conv_spec = lambda part: pl.BlockSpec((GDN_CONV, GDN_WIDTH), lambda bi: (0, part))
    row_spec = pl.BlockSpec((1, LANES), lambda bi: (0, 0))
    per_head = lambda dtype: pltpu.VMEM((GDN_HEADS, t, LANES), dtype)
    return pl.pallas_call(
        functools.partial(_gdn_kernel, t=t),
        grid=(b,),
        in_specs=[
            wide(COL_GDN_Q), wide(COL_GDN_K), wide(COL_GDN_V), wide(COL_GDN_Z),
            pl.BlockSpec((t, LANES), lambda bi: (bi, 0)),
            conv_spec(0), conv_spec(1), conv_spec(2),
            row_spec, row_spec, row_spec,
        ],
        out_specs=pl.BlockSpec((t, GDN_WIDTH), lambda bi: (bi, 0)),
        out_shape=jax.ShapeDtypeStruct((b * t, GDN_WIDTH), BF16),
        scratch_shapes=([pltpu.VMEM((t, LANES), F32) for _ in range(5)]
                        + [per_head(F32), pltpu.VMEM((GDN_HEADS, t // GDN_CHUNK * SUBLANES, LANES), F32)]
                        + [per_head(BF16) for _ in range(4)]),
        compiler_params=_params("arbitrary"),
        name="gdn",
    )(proj, proj, proj, proj, ba, conv_w, conv_w, conv_w, alog_row, dtb_row, gn_row)


def _merge_kernel(x_ref, g0_ref, g1_ref, g2_ref, ysb_ref, yret_ref, ygdn_ref, wsb_ref, wret_ref, wgdn_ref,
                  wout_ref, fg_ref, o_ref, *, final):
    def branch(g_ref, y_ref, w_ref):
        return _sigmoid(g_ref[...].astype(F32)) * _dot(y_ref[...], w_ref[...])

    merged = (branch(g0_ref, ysb_ref, wsb_ref) + branch(g1_ref, yret_ref, wret_ref)
              + branch(g2_ref, ygdn_ref, wgdn_ref))
    x = x_ref[...] + _dot(merged.astype(BF16), wout_ref[...])
    if final:
        x = x * lax.rsqrt(jnp.mean(x * x, axis=-1, keepdims=True) + EPS) * fg_ref[...]
    o_ref[...] = x


def _merge_out(x2, proj, ysb, yret, ygdn, wsb, wret, wgdn, wout, final_g, final):
    m = x2.shape[0]
    tm = min(512, m)
    tile = lambda width, j=0: pl.BlockSpec((tm, width), lambda i: (i, j))
    whole = lambda a: pl.BlockSpec(a.shape, lambda i: (0, 0))
    return pl.pallas_call(
        functools.partial(_merge_kernel, final=final),
        grid=(m // tm,),
        in_specs=[
            tile(D_MODEL), tile(D_MODEL, 0), tile(D_MODEL, 1), tile(D_MODEL, 2),
            tile(SB_WIDTH), tile(RET_WIDTH), tile(GDN_WIDTH),
            whole(wsb), whole(wret), whole(wgdn), whole(wout), whole(final_g),
        ],
        out_specs=tile(D_MODEL),
        out_shape=jax.ShapeDtypeStruct((m, D_MODEL), F32),
        compiler_params=_params("arbitrary"),
        name="merge_out",
    )(x2, proj, proj, proj, ysb, yret, ygdn, wsb, wret, wgdn, wout, final_g)


def _prep_in_weights(w_in):
    sizes = (SB_WIDTH, SB_WIDTH, SB_WIDTH, SB_WIDTH, RET_QK_WIDTH, RET_QK_WIDTH, RET_WIDTH, RET_WIDTH,
             3 * GDN_WIDTH, GDN_WIDTH)
    scales = [1.0] * len(sizes)
    scales[0] = SB_HEAD_DIM ** -0.5 * LOG2E
    scales[5] = RET_QK_DIM ** -0.5
    parts, start = [w_in[..., _REF_GATE:]], 0
    for size, scale in zip(sizes, scales):
        piece = w_in[..., start:start + size]
        parts.append(piece if scale == 1.0 else piece * scale)
        start += size
    w_main = jnp.concatenate(parts, axis=-1).astype(BF16)
    w_ba = w_in[..., _REF_BA:_REF_BA + 2 * GDN_HEADS]
    w_ba = jnp.pad(w_ba, ((0, 0), (0, 0), (0, LANES - 2 * GDN_HEADS))).astype(BF16)
    return w_main, w_ba


def kernel(x, norm_g, w_in, conv_w, a_log, dt_bias, gdn_norm_g, w_sb, w_ret, w_gdn, w_out, final_g):
    b, t, d = x.shape
    depth = w_in.shape[0]
    assert d == D_MODEL and t % SB_TILE == 0 and t % RET_CHUNK == 0 and t % GDN_CHUNK == 0
    w_main, w_ba = _prep_in_weights(w_in)
    w_sb, w_ret, w_gdn, w_out = (w.astype(BF16) for w in (w_sb, w_ret, w_gdn, w_out))
    pad_heads = lambda v: jnp.pad(v, ((0, 0), (GDN_HEADS, LANES - 2 * GDN_HEADS)))[:, None, :]
    alog_rows, dtb_rows = pad_heads(a_log.astype(F32)), pad_heads(dt_bias.astype(F32))
    cos_t, sin_t = _rope_tables(t)
    final_row = final_g.astype(F32)[None, :]

    x2 = x.reshape(b * t, d)
    for layer in range(depth):
        proj, ba = _in_proj(x2, norm_g[layer][None, :], w_main[layer], w_ba[layer])
        ysb = _sb_attention(proj, b, t)
        yret = _retention(proj, cos_t, sin_t, b, t)
        ygdn = _gdn(proj, ba, conv_w[layer], alog_rows[layer], dtb_rows[layer], gdn_norm_g[layer][None, :], b, t)
        x2 = _merge_out(x2, proj, ysb, yret, ygdn, w_sb[layer], w_ret[layer], w_gdn[layer], w_out[layer],
                        final_row, final=(layer == depth - 1))
    return x2.reshape(b, t, d)
```

```python
import functools
import math

import jax
import jax.numpy as jnp
from jax import lax
from jax.experimental import pallas as pl
from jax.experimental.pallas import tpu as pltpu

F32 = jnp.float32
BF16 = jnp.bfloat16

D_MODEL = 1024
EPS = 1e-6
LOG2E = math.log2(math.e)
LANES = 128
SUBLANES = 8
VMEM_LIMIT = 56 * 1024 * 1024

SB_HEAD_DIM = 64
SB_WIDTH = 512
SB_TILE = 256
RET_QK_DIM = 64
RET_HEADS = 4
RET_QK_WIDTH = 256
RET_WIDTH = 512
RET_CHUNK = 256
ROPE_BASE = 10000.0
GDN_HEADS = 4
GDN_HEAD_DIM = 128
GDN_WIDTH = 512
GDN_CONV = 4
GDN_CHUNK = 128
GDN_INV_BASE = 16
GDN_PREP_GROUP = 16
N_BRANCH = 3

COL_GATE = 0
COL_SB_Q, COL_SB_K, COL_SB_V, COL_SB_Z = 24, 28, 32, 36
COL_RET_Q, COL_RET_K, COL_RET_V, COL_RET_Z = 40, 42, 44, 48
COL_GDN_Q, COL_GDN_K, COL_GDN_V, COL_GDN_Z = 52, 56, 60, 64
PROJ_WIDTH = 68 * LANES
PROJ_TILE_N = 512
_REF_BA = 5632
_REF_GATE = 5640


def _dot(a, b, contract=((1,), (0,)), precision=None):
    return lax.dot_general(a, b, (contract, ((), ())), precision=precision,
                           preferred_element_type=F32)


def _dot_nt(a, b, precision=None):
    return _dot(a, b, ((1,), (1,)), precision)


def _dot_tn(a, b, precision=None):
    return _dot(a, b, ((0,), (0,)), precision)


def _split_bf16(a):
    hi = a.astype(BF16)
    return hi, (a - hi.astype(F32)).astype(BF16)


def _dot_inv(a, b):
    return _dot(a.astype(BF16), b.astype(BF16))


def _sigmoid(x):
    return 1.0 / (1.0 + jnp.exp2(x * -LOG2E))


def _softplus(x):
    return jnp.maximum(x, 0.0) + jnp.log(1.0 + jnp.exp(-jnp.abs(x)))


def _params(*sem):
    return pltpu.CompilerParams(dimension_semantics=sem, vmem_limit_bytes=VMEM_LIMIT)


def _in_proj_kernel(x_ref, g_ref, w_ref, wba_ref, proj_ref, ba_ref, hn_ref):
    @pl.when(pl.program_id(1) == 0)
    def _():
        x = x_ref[...]
        ms = jnp.mean(x * x, axis=-1, keepdims=True)
        hn = (x * lax.rsqrt(ms + EPS) * g_ref[...]).astype(BF16)
        hn_ref[...] = hn
        ba_ref[...] = _dot_nt(hn, wba_ref[...])

    proj_ref[...] = _dot_nt(hn_ref[...], w_ref[...]).astype(BF16)


def _in_proj(x2, gain, w, wba):
    m = x2.shape[0]
    tm = min(2048, m)
    return pl.pallas_call(
        _in_proj_kernel,
        grid=(m // tm, PROJ_WIDTH // PROJ_TILE_N),
        in_specs=[
            pl.BlockSpec((tm, D_MODEL), lambda i, j: (i, 0)),
            pl.BlockSpec((1, D_MODEL), lambda i, j: (0, 0)),
            pl.BlockSpec((PROJ_TILE_N, D_MODEL), lambda i, j: (j, 0)),
            pl.BlockSpec((LANES, D_MODEL), lambda i, j: (0, 0)),
        ],
        out_specs=[
            pl.BlockSpec((tm, PROJ_TILE_N), lambda i, j: (i, j)),
            pl.BlockSpec((tm, LANES), lambda i, j: (i, 0)),
        ],
        out_shape=[
            jax.ShapeDtypeStruct((m, PROJ_WIDTH), BF16),
            jax.ShapeDtypeStruct((m, LANES), F32),
        ],
        scratch_shapes=[pltpu.VMEM((tm, D_MODEL), BF16)],
        compiler_params=_params("arbitrary", "arbitrary"),
        name="in_proj",
    )(x2, gain, w, wba)


def _sb_kernel(q_ref, k_ref, v_ref, z_ref, o_ref):
    tq = SB_TILE
    n_pairs = SB_WIDTH // LANES
    qi = pl.program_id(1)
    lane = lax.broadcasted_iota(jnp.int32, (1, LANES), 1)
    row = lax.broadcasted_iota(jnp.int32, (tq, tq), 0)
    col = lax.broadcasted_iota(jnp.int32, (tq, tq), 1)
    tri = jnp.where(row > col, 1.0, 0.0).astype(BF16)
    row2 = lax.broadcasted_iota(jnp.int32, (2 * tq, tq), 0) % tq
    col2 = lax.broadcasted_iota(jnp.int32, (2 * tq, tq), 1)
    valid = col2 < row2
    lanes_of = lambda p: slice(p * LANES, (p + 1) * LANES)

    q = q_ref[...]
    zero = jnp.zeros((tq, LANES), BF16)
    q_pairs = []
    for p in range(n_pairs):
        qp = q[:, lanes_of(p)]
        q_pairs.append(jnp.concatenate(
            [jnp.where(lane < SB_HEAD_DIM, qp, zero), jnp.where(lane >= SB_HEAD_DIM, qp, zero)], axis=0))

    def block(start, state, masked):
        kb = k_ref[pl.ds(start, tq), :]
        vb = v_ref[pl.ds(start, tq), :]
        carry, acc = state
        log_sig, sp_bf16, rowsum, w, new_acc = {}, {}, {}, {}, {}

        def logits(p):
            z = _dot_nt(q_pairs[p], kb[:, lanes_of(p)])
            log_sig[p] = jnp.minimum(z, 0.0) - jnp.log2(1.0 + jnp.exp2(-jnp.abs(z)))
            sp = z - log_sig[p]
            if masked:
                sp = jnp.where(valid, sp, 0.0)
            sp_bf16[p] = sp.astype(BF16)
            rowsum[p] = jnp.sum(sp, axis=-1, keepdims=True)

        def weights(p):
            after = _dot(sp_bf16[p], tri)
            w[p] = jnp.exp2(log_sig[p] - after - carry[p])
            if masked:
                w[p] = jnp.where(valid, w[p], 0.0)

        def values(p):
            new_acc[p] = acc[p] + _dot(w[p].astype(BF16), vb[:, lanes_of(p)])

        stages = (logits, weights, values)
        for step in range(n_pairs + len(stages) - 1):
            for depth, stage in enumerate(stages):
                if 0 <= step - depth < n_pairs:
                    stage(step - depth)
        return ([carry[p] + rowsum[p] for p in range(n_pairs)], [new_acc[p] for p in range(n_pairs)])

    init = ([jnp.zeros((2 * tq, 1), F32)] * n_pairs, [jnp.zeros((2 * tq, LANES), F32)] * n_pairs)
    state = block(pl.multiple_of(qi * tq, tq), init, True)

    def body(i, state):
        return block(pl.multiple_of((qi - 1 - i) * tq, tq), state, False)

    _, acc = lax.fori_loop(0, qi, body, state)
    o = jnp.concatenate([jnp.where(lane < SB_HEAD_DIM, acc[p][:tq, :], acc[p][tq:, :]) for p in range(n_pairs)],
                        axis=1)
    zg = z_ref[...].astype(F32)
    o_ref[...] = (o * (zg * _sigmoid(zg))).astype(BF16)


def _sb_attention(proj, b, t):
    tq = SB_TILE
    nq = t // tq
    return pl.pallas_call(
        _sb_kernel,
        grid=(b, nq),
        in_specs=[
            pl.BlockSpec((tq, SB_WIDTH), lambda bi, qi: (bi * nq + qi, COL_SB_Q * LANES // SB_WIDTH)),
            pl.BlockSpec((t, SB_WIDTH), lambda bi, qi: (bi, COL_SB_K * LANES // SB_WIDTH)),
            pl.BlockSpec((t, SB_WIDTH), lambda bi, qi: (bi, COL_SB_V * LANES // SB_WIDTH)),
            pl.BlockSpec((tq, SB_WIDTH), lambda bi, qi: (bi * nq + qi, COL_SB_Z * LANES // SB_WIDTH)),
        ],
        out_specs=pl.BlockSpec((tq, SB_WIDTH), lambda bi, qi: (bi * nq + qi, 0)),
        out_shape=jax.ShapeDtypeStruct((b * t, SB_WIDTH), BF16),
        compiler_params=_params("arbitrary", "arbitrary"),
        name="sb_attn",
    )(proj, proj, proj, proj)


def _ret_kernel(q_ref, k_ref, v_ref, z_ref, cos_ref, sin_ref, o_ref, qr_ref, kr_ref, *, t):
    c = RET_CHUNK
    n_chunks = t // c
    lane = lax.broadcasted_iota(jnp.int32, (1, LANES), 1)
    first_half = (lane % RET_QK_DIM) < (RET_QK_DIM // 2)
    row = lax.broadcasted_iota(jnp.int32, (c, c), 0)
    col = lax.broadcasted_iota(jnp.int32, (c, c), 1)
    diff = (row - col).astype(F32)
    pos = lax.broadcasted_iota(jnp.int32, (c, LANES), 0).astype(F32)

    def rotate(x):
        half = RET_QK_DIM // 2
        other = jnp.where(first_half, pltpu.roll(x, LANES - half, 1), pltpu.roll(x, half, 1))
        return x * cos_ref[...] + other * sin_ref[...]

    for pair in range(RET_HEADS // 2):
        cols = slice(pair * LANES, (pair + 1) * LANES)
        qr_ref[pair] = rotate(q_ref[:, cols].astype(F32))
        kr_ref[pair] = rotate(k_ref[:, cols].astype(F32))

    heads = list(range(RET_HEADS))
    log_gamma = [math.log(1.0 - 2.0 ** (-5.0 - h)) for h in heads]
    head_lanes = [(lane >= RET_QK_DIM) if h % 2 else (lane < RET_QK_DIM) for h in heads]
    intra_decay = [jnp.where(diff >= 0.0, jnp.exp(lg * jnp.maximum(diff, 0.0)), 0.0) for lg in log_gamma]
    q_decay = [jnp.exp(lg * (pos + 1.0)) for lg in log_gamma]
    k_decay = [jnp.exp(lg * (c - 1.0 - pos)) for lg in log_gamma]
    chunk_decay = [math.exp(lg * c) for lg in log_gamma]
    vcols = [slice(h * LANES, (h + 1) * LANES) for h in heads]

    def each(fn, *lists):
        return [fn(*args) for args in zip(*lists)]

    def chunk(i, states):
        rows = pl.ds(pl.multiple_of(i * c, c), c)
        qc = each(lambda h, m: jnp.where(m, qr_ref[h // 2, rows, :], 0.0), heads, head_lanes)
        kc = each(lambda h, m: jnp.where(m, kr_ref[h // 2, rows, :], 0.0), heads, head_lanes)
        vc = each(lambda cols: v_ref[rows, cols], vcols)
        scores = each(lambda q, k, d: (_dot_nt(q.astype(BF16), k.astype(BF16)) * d).astype(BF16), qc, kc, intra_decay)
        inter = each(lambda q, d, s: _dot((q * d).astype(BF16), s.astype(BF16)), qc, q_decay, states)
        states = each(lambda s, cd, k, d, v: s * cd + _dot_tn((k * d).astype(BF16), v),
                      states, chunk_decay, kc, k_decay, vc)
        o = each(lambda sc, v, x: _dot(sc, v) + x, scores, vc, inter)
        for cols, x in zip(vcols, o):
            xc = x - jnp.mean(x, axis=-1, keepdims=True)
            on = xc * lax.rsqrt(jnp.mean(xc * xc, axis=-1, keepdims=True) + EPS)
            zg = z_ref[rows, cols].astype(F32)
            o_ref[rows, cols] = (on * (zg * _sigmoid(zg))).astype(BF16)
        return states

    lax.fori_loop(0, n_chunks, chunk, [jnp.zeros((LANES, LANES), F32)] * RET_HEADS)


def _retention(proj, cos_t, sin_t, b, t):
    return pl.pallas_call(
        functools.partial(_ret_kernel, t=t),
        grid=(b,),
        in_specs=[
            pl.BlockSpec((t, RET_QK_WIDTH), lambda bi: (bi, COL_RET_Q * LANES // RET_QK_WIDTH)),
            pl.BlockSpec((t, RET_QK_WIDTH), lambda bi: (bi, COL_RET_K * LANES // RET_QK_WIDTH)),
            pl.BlockSpec((t, RET_WIDTH), lambda bi: (bi, COL_RET_V * LANES // RET_WIDTH)),
            pl.BlockSpec((t, RET_WIDTH), lambda bi: (bi, COL_RET_Z * LANES // RET_WIDTH)),
            pl.BlockSpec((t, LANES), lambda bi: (0, 0)),
            pl.BlockSpec((t, LANES), lambda bi: (0, 0)),
        ],
        out_specs=pl.BlockSpec((t, RET_WIDTH), lambda bi: (bi, 0)),
        out_shape=jax.ShapeDtypeStruct((b * t, RET_WIDTH), BF16),
        scratch_shapes=[pltpu.VMEM((RET_HEADS // 2, t, LANES), F32) for _ in range(2)],
        compiler_params=_params("arbitrary"),
        name="retention",
    )(proj, proj, proj, proj, cos_t, sin_t)


def _rope_tables(t):
    half = RET_QK_DIM // 2
    inv_freq = ROPE_BASE ** (-jnp.arange(half, dtype=F32) / half)
    ang = jnp.arange(t).astype(F32)[:, None] * inv_freq[None, :]
    cos, sin = jnp.cos(ang), jnp.sin(ang)
    return jnp.tile(cos, (1, 4)), jnp.tile(jnp.concatenate([-sin, sin], axis=-1), (1, 2))


def _gdn_kernel(q_ref, k_ref, v_ref, z_ref, ba_ref, cwq_ref, cwk_ref, cwv_ref, alog_ref, dtb_ref, gn_ref,
                o_ref, qs_ref, ks_ref, vs_ref, beta_ref, g_ref, xpad_ref, u_ref, elast_ref, w_ref, sc_ref, qd_ref,
                kd_ref, *, t):
    c = GDN_CHUNK
    lane = lax.broadcasted_iota(jnp.int32, (1, LANES), 1)
    lanes_of = lambda h: slice(h * LANES, (h + 1) * LANES)

    def conv_silu(x_ref, w_ref, h):
        x = x_ref[:, lanes_of(h)].astype(F32)
        w = w_ref[:, lanes_of(h)]
        xpad_ref[pl.ds(0, SUBLANES), :] = jnp.zeros((SUBLANES, LANES), F32)
        xpad_ref[pl.ds(SUBLANES, t), :] = x
        y = x * w[GDN_CONV - 1:GDN_CONV, :]
        for s in range(1, GDN_CONV):
            y = y + xpad_ref[pl.ds(SUBLANES - s, t), :] * w[GDN_CONV - 1 - s:GDN_CONV - s, :]
        return y * _sigmoid(y)

    def l2norm(x, scale=1.0):
        return x * (lax.rsqrt(jnp.sum(x * x, axis=-1, keepdims=True) + EPS) * scale)

    row = lax.broadcasted_iota(jnp.int32, (c, c), 0)
    col = lax.broadcasted_iota(jnp.int32, (c, c), 1)
    incl = row >= col
    strict = row > col
    tril = jnp.where(incl, 1.0, 0.0).astype(BF16)
    base_blocks = (row // GDN_INV_BASE) == (col // GDN_INV_BASE)
    sibling_blocks = []
    size = GDN_INV_BASE
    while size < c:
        sibling_blocks.append(((row // size) % 2 == 1) & ((col // size) == (row // size) - 1))
        size *= 2
    gain = gn_ref[...]
    prep_in = (qs_ref, ks_ref, vs_ref, beta_ref, g_ref)

    def each(fn, *lists):
        return [fn(*args) for args in zip(*lists)]

    def prepare_group(i, head):
        first = i * GDN_PREP_GROUP
        rows = [pl.ds(pl.multiple_of((first + j) * c, c), c) for j in range(GDN_PREP_GROUP)]
        qc, kc, vc, bb, g = ([ref[r, :] for r in rows] for ref in prep_in)
        g_split = each(_split_bf16, g)
        gc = each(lambda s: _dot(tril, s[0]) + _dot(tril, s[1]), g_split)
        decay = each(lambda x: jnp.where(incl, jnp.exp(jnp.where(incl, x - x.T, 0.0)), 0.0), gc)
        kb = each(lambda k, b_: k * b_, kc, bb)
        kcb = each(lambda k: k.astype(BF16), kc)
        neg_l = each(lambda k_b, k_c, d: -jnp.where(strict, _dot_nt(k_b.astype(BF16), k_c) * d, 0.0), kb, kcb, decay)
        power = each(lambda n: jnp.where(base_blocks, n, 0.0), neg_l)
        corr = power
        for _ in range(int(math.log2(GDN_INV_BASE)) - 1):
            power = each(lambda p: _dot_inv(p, p), power)
            corr = each(lambda m, p: m + p + _dot_inv(m, p), corr, power)
        for sib in sibling_blocks:
            c_off = each(lambda n: jnp.where(sib, n, 0.0), neg_l)
            left = each(lambda m, x: x + _dot_inv(m, x), corr, c_off)
            corr = each(lambda m, x: m + x + _dot_inv(x, m), corr, left)
        eg = each(jnp.exp, gc)
        vb = each(lambda v, b_: v * b_, vc, bb)
        kbe = each(lambda k_b, e: k_b * e, kb, eg)
        outs = (
            (u_ref, each(lambda m, x: x + _dot_inv(m, x), corr, vb)),
            (w_ref, each(lambda m, x: (x + _dot_inv(m, x)).astype(BF16), corr, kbe)),
            (sc_ref, each(lambda q, k_c, d: (_dot_nt(q.astype(BF16), k_c) * d).astype(BF16), qc, kcb, decay)),
            (qd_ref, each(lambda q, e: (q * e).astype(BF16), qc, eg)),
            (kd_ref, each(lambda k, x: (k * jnp.exp(x[c - 1:c, :] - x)).astype(BF16), kc, gc)),
        )
        for ref, vals in outs:
            for r, val in zip(rows, vals):
                ref[head, r, :] = val
        for j, x in enumerate(gc):
            elast_ref[head, pl.ds(pl.multiple_of((first + j) * SUBLANES, SUBLANES), SUBLANES), :] = (
                jnp.broadcast_to(jnp.exp(x[c - 1:c, :]), (SUBLANES, LANES)))
        return head

    ba = ba_ref[...]
    beta_all = _sigmoid(ba)
    g_all = -jnp.exp(alog_ref[...]) * _softplus(ba + dtb_ref[...])
    for head in range(GDN_HEADS):
        qs_ref[...] = l2norm(conv_silu(q_ref, cwq_ref, head), GDN_HEAD_DIM ** -0.5)
        ks_ref[...] = l2norm(conv_silu(k_ref, cwk_ref, head))
        vs_ref[...] = conv_silu(v_ref, cwv_ref, head)
        beta_ref[...] = jnp.broadcast_to(
            jnp.sum(jnp.where(lane == head, beta_all, 0.0), axis=-1, keepdims=True), (t, LANES))
        g_ref[...] = jnp.broadcast_to(
            jnp.sum(jnp.where(lane == head + GDN_HEADS, g_all, 0.0), axis=-1, keepdims=True), (t, LANES))
        lax.fori_loop(0, t // (c * GDN_PREP_GROUP), prepare_group, head)

    heads = list(range(GDN_HEADS))

    def scan(i, states):
        rows = pl.ds(pl.multiple_of(i * c, c), c)
        last_rows = pl.ds(pl.multiple_of(i * SUBLANES, SUBLANES), SUBLANES)
        sb = each(lambda s: s.astype(BF16), states)
        vnb = each(lambda h, s: (u_ref[h, rows, :] - _dot(w_ref[h, rows, :], s)).astype(BF16), heads, sb)
        o = each(lambda h, s, v: _dot(qd_ref[h, rows, :], s) + _dot(sc_ref[h, rows, :], v), heads, sb, vnb)
        states = each(lambda h, s, v: s * elast_ref[h, last_rows, :][:1, :] + _dot_tn(kd_ref[h, rows, :], v),
                      heads, states, vnb)
        for h, x in zip(heads, o):
            on = x * lax.rsqrt(jnp.mean(x * x, axis=-1, keepdims=True) + EPS) * gain
            zg = z_ref[rows, lanes_of(h)].astype(F32)
            o_ref[rows, lanes_of(h)] = (on * (zg * _sigmoid(zg))).astype(BF16)
        return states

    lax.fori_loop(0, t // c, scan, [jnp.zeros((LANES, LANES), F32)] * GDN_HEADS)


def _gdn(proj, ba, conv_w, alog_row, dtb_row, gn_row, b, t):
    wide = lambda base: pl.BlockSpec((t, GDN_WIDTH), lambda bi: (bi, base * LANES // GDN_WIDTH))
    conv_spec = lambda part: pl.BlockSpec((GDN_CONV, GDN_WIDTH), lambda bi: (0, part))
    row_spec = pl.BlockSpec((1, LANES), lambda bi: (0, 0))
    per_head = lambda dtype: pltpu.VMEM((GDN_HEADS, t, LANES), dtype)
    return pl.pallas_call(
        functools.partial(_gdn_kernel, t=t),
        grid=(b,),
        in_specs=[
            wide(COL_GDN_Q), wide(COL_GDN_K), wide(COL_GDN_V), wide(COL_GDN_Z),
            pl.BlockSpec((t, LANES), lambda bi: (bi, 0)),
            conv_spec(0), conv_spec(1), conv_spec(2),
            row_spec, row_spec, row_spec,
        ],
        out_specs=pl.BlockSpec((t, GDN_WIDTH), lambda bi: (bi, 0)),
        out_shape=jax.ShapeDtypeStruct((b * t, GDN_WIDTH), BF16),
        scratch_shapes=([pltpu.VMEM((t, LANES), F32) for _ in range(5)]
                        + [pltpu.VMEM((t + SUBLANES, LANES), F32)]
                        + [per_head(F32),pltpu.VMEM((GDN_HEADS, t // GDN_CHUNK * SUBLANES, LANES), F32)]
                        + [per_head(BF16) for _ in range(4)]),
        compiler_params=_params("arbitrary"),
        name="gdn",
    )(proj, proj, proj, proj, ba, conv_w, conv_w, conv_w, alog_row, dtb_row, gn_row)


def _merge_kernel(x_ref, g0_ref, g1_ref, g2_ref, ysb_ref, yret_ref, ygdn_ref, wsb_ref, wret_ref, wgdn_ref,
                  wout_ref, fg_ref, o_ref, *, final):
    def branch(g_ref, y_ref, w_ref):
        return _sigmoid(g_ref[...].astype(F32)) * _dot(y_ref[...], w_ref[...])

    merged = (branch(g0_ref, ysb_ref, wsb_ref) + branch(g1_ref, yret_ref, wret_ref)
              + branch(g2_ref, ygdn_ref, wgdn_ref))
    x = x_ref[...] + _dot(merged.astype(BF16), wout_ref[...])
    if final:
        x = x * lax.rsqrt(jnp.mean(x * x, axis=-1, keepdims=True) + EPS) * fg_ref[...]
    o_ref[...] = x


def _merge_out(x2, proj, ysb, yret, ygdn, wsb, wret, wgdn, wout, final_g, final):
    m = x2.shape[0]
    tm = min(512, m)
    tile = lambda width, j=0: pl.BlockSpec((tm, width), lambda i: (i, j))
    whole = lambda a: pl.BlockSpec(a.shape, lambda i: (0, 0))
    return pl.pallas_call(
        functools.partial(_merge_kernel, final=final),
        grid=(m // tm,),
        in_specs=[
            tile(D_MODEL), tile(D_MODEL, 0), tile(D_MODEL, 1), tile(D_MODEL, 2),
            tile(SB_WIDTH), tile(RET_WIDTH), tile(GDN_WIDTH),
            whole(wsb), whole(wret), whole(wgdn), whole(wout), whole(final_g),
        ],
        out_specs=tile(D_MODEL),
        out_shape=jax.ShapeDtypeStruct((m, D_MODEL), F32),
        compiler_params=_params("arbitrary"),
        name="merge_out",
    )(x2, proj, proj, proj, ysb, yret, ygdn, wsb, wret, wgdn, wout, final_g)


def _prep_in_weights(w_in):
    sizes = (SB_WIDTH, SB_WIDTH, SB_WIDTH, SB_WIDTH, RET_QK_WIDTH, RET_QK_WIDTH, RET_WIDTH, RET_WIDTH,
             3 * GDN_WIDTH, GDN_WIDTH)
    scales = [1.0] * len(sizes)
    scales[0] = SB_HEAD_DIM ** -0.5 * LOG2E
    scales[5] = RET_QK_DIM ** -0.5
    w_t = jnp.swapaxes(w_in, 1, 2)
    parts, start = [w_t[:, _REF_GATE:, :]], 0
    for size, scale in zip(sizes, scales):
        piece = w_t[:, start:start + size, :]
        parts.append(piece if scale == 1.0 else piece * scale)
        start += size
    w_main = jnp.concatenate(parts, axis=1).astype(BF16)
    w_ba = w_t[:, _REF_BA:_REF_BA + 2 * GDN_HEADS, :]
    w_ba = jnp.pad(w_ba, ((0, 0), (0, LANES - 2 * GDN_HEADS), (0, 0))).astype(BF16)
    return w_main, w_ba


def kernel(x, norm_g, w_in, conv_w, a_log, dt_bias, gdn_norm_g, w_sb, w_ret, w_gdn, w_out, final_g):
    b, t, d = x.shape
    depth = w_in.shape[0]
    assert d == D_MODEL and t % SB_TILE == 0 and t % RET_CHUNK == 0 and t % GDN_CHUNK == 0
    w_main, w_ba = _prep_in_weights(w_in)
    w_sb, w_ret, w_gdn, w_out = (w.astype(BF16) for w in (w_sb, w_ret, w_gdn, w_out))
    pad_heads = lambda v: jnp.pad(v, ((0, 0), (GDN_HEADS, LANES - 2 * GDN_HEADS)))[:, None, :]
    alog_rows, dtb_rows = pad_heads(a_log.astype(F32)), pad_heads(dt_bias.astype(F32))
    cos_t, sin_t = _rope_tables(t)
    final_row = final_g.astype(F32)[None, :]

    x2 = x.reshape(b * t, d)
    for layer in range(depth):
        proj, ba = _in_proj(x2, norm_g[layer][None, :], w_main[layer], w_ba[layer])
        ysb = _sb_attention(proj, b, t)
        yret = _retention(proj, cos_t, sin_t, b, t)
        ygdn = _gdn(proj, ba, conv_w[layer], alog_rows[layer], dtb_rows[layer], gdn_norm_g[layer][None, :], b, t)
        x2 = _merge_out(x2, proj, ysb, yret, ygdn, w_sb[layer], w_ret[layer], w_gdn[layer], w_out[layer],
                        final_row, final=(layer == depth - 1))
    return x2.reshape(b, t, d)
```

```python
import functools
import math

import jax
import jax.numpy as jnp
import numpy as np
from jax import lax
from jax.experimental import pallas as pl
from jax.experimental.pallas import tpu as pltpu

F32 = jnp.float32
BF16 = jnp.bfloat16

D_MODEL = 1024
EPS = 1e-6
LOG2E = math.log2(math.e)
LANES = 128
SUBLANES = 8
VMEM_LIMIT = 56 * 1024 * 1024

SB_HEAD_DIM = 64
SB_WIDTH = 512
SB_TILE = 256
RET_QK_DIM = 64
RET_HEADS = 4
RET_QK_WIDTH = 256
RET_WIDTH = 512
RET_CHUNK = 256
ROPE_BASE = 10000.0
GDN_HEADS = 4
GDN_HEAD_DIM = 128
GDN_WIDTH = 512
GDN_CONV = 4
GDN_CHUNK = 128
GDN_INV_BASE = 16
GDN_PREP_GROUP = 16
N_BRANCH = 3

COL_GATE = 0
COL_SB_Q, COL_SB_K, COL_SB_V, COL_SB_Z = 24, 28, 32, 36
COL_RET_Q, COL_RET_K, COL_RET_V, COL_RET_Z = 40, 42, 44, 48
COL_GDN_Q, COL_GDN_K, COL_GDN_V, COL_GDN_Z = 52, 56, 60, 64
PROJ_WIDTH = 68 * LANES
PROJ_TILE_N = 512
_REF_BA = 5632
_REF_GATE = 5640


def _dot(a, b, contract=((1,), (0,)), precision=None):
    return lax.dot_general(a, b, (contract, ((), ())), precision=precision,
                           preferred_element_type=F32)


def _dot_nt(a, b, precision=None):
    return _dot(a, b, ((1,), (1,)), precision)


def _dot_tn(a, b, precision=None):
    return _dot(a, b, ((0,), (0,)), precision)


def _split_bf16(a):
    hi = a.astype(BF16)
    return hi, (a - hi.astype(F32)).astype(BF16)


def _dot_inv(a, b):
    return _dot(a.astype(BF16), b.astype(BF16))


def _sigmoid(x):
    return 1.0 / (1.0 + jnp.exp2(x * -LOG2E))


def _softplus(x):
    return jnp.maximum(x, 0.0) + jnp.log(1.0 + jnp.exp(-jnp.abs(x)))


def _params(*sem):
    return pltpu.CompilerParams(dimension_semantics=sem, vmem_limit_bytes=VMEM_LIMIT)


def _in_proj_kernel(x_ref, g_ref, w_ref, wba_ref, scale_ref, proj_ref, ba_ref, hn_ref):
    @pl.when(pl.program_id(1) == 0)
    def _():
        x = x_ref[...]
        ms = jnp.mean(x * x, axis=-1, keepdims=True)
        hn = (x * lax.rsqrt(ms + EPS) * g_ref[...]).astype(BF16)
        hn_ref[...] = hn
        ba_ref[...] = _dot_nt(hn, wba_ref[...])

    proj_ref[...] = (_dot_nt(hn_ref[...], w_ref[...]) * scale_ref[:1, :]).astype(BF16)


def _proj_scales():
    scales = np.ones((PROJ_WIDTH,), np.float32)
    scales[COL_SB_Q * LANES:COL_SB_Q * LANES + SB_WIDTH] = SB_HEAD_DIM ** -0.5 * LOG2E
    scales[COL_RET_K * LANES:COL_RET_K * LANES + RET_QK_WIDTH] = RET_QK_DIM ** -0.5
    return jnp.asarray(np.repeat(scales.reshape(-1, 1, PROJ_TILE_N), SUBLANES, axis=1).reshape(-1, PROJ_TILE_N))


def _in_proj(x2, gain, w_all, wba_all, scales, layer):
    m = x2.shape[0]
    tm = min(2048, m)
    return pl.pallas_call(
        _in_proj_kernel,
        grid=(m // tm, PROJ_WIDTH // PROJ_TILE_N),
        in_specs=[
            pl.BlockSpec((tm, D_MODEL), lambda i, j: (i, 0)),
            pl.BlockSpec((1, D_MODEL), lambda i, j: (0, 0)),
            pl.BlockSpec((None, PROJ_TILE_N, D_MODEL), lambda i, j: (layer, j, 0)),
            pl.BlockSpec((None, LANES, D_MODEL), lambda i, j: (layer, 0, 0)),
            pl.BlockSpec((SUBLANES, PROJ_TILE_N), lambda i, j: (j, 0)),
        ],
        out_specs=[
            pl.BlockSpec((tm, PROJ_TILE_N), lambda i, j: (i, j)),
            pl.BlockSpec((tm, LANES), lambda i, j: (i, 0)),
        ],
        out_shape=[
            jax.ShapeDtypeStruct((m, PROJ_WIDTH), BF16),
            jax.ShapeDtypeStruct((m, LANES), F32),
        ],
        scratch_shapes=[pltpu.VMEM((tm, D_MODEL), BF16)],
        compiler_params=_params("arbitrary", "arbitrary"),
        name="in_proj",
    )(x2, gain, w_all, wba_all, scales)


def _sb_kernel(q_ref, k_ref, v_ref, z_ref, o_ref):
    tq = SB_TILE
    n_pairs = SB_WIDTH // LANES
    qi = pl.program_id(1)
    lane = lax.broadcasted_iota(jnp.int32, (1, LANES), 1)
    row = lax.broadcasted_iota(jnp.int32, (tq, tq), 0)
    col = lax.broadcasted_iota(jnp.int32, (tq, tq), 1)
    tri = jnp.where(row > col, 1.0, 0.0).astype(BF16)
    row2 = lax.broadcasted_iota(jnp.int32, (2 * tq, tq), 0) % tq
    col2 = lax.broadcasted_iota(jnp.int32, (2 * tq, tq), 1)
    valid = col2 < row2
    lanes_of = lambda p: slice(p * LANES, (p + 1) * LANES)

    q = q_ref[...]
    zero = jnp.zeros((tq, LANES), BF16)
    q_pairs = []
    for p in range(n_pairs):
        qp = q[:, lanes_of(p)]
        q_pairs.append(jnp.concatenate(
            [jnp.where(lane < SB_HEAD_DIM, qp, zero), jnp.where(lane >= SB_HEAD_DIM, qp, zero)], axis=0))

    def blocks(starts, state, masked):
        kb = [k_ref[pl.ds(start, tq), :] for start in starts]
        vb = [v_ref[pl.ds(start, tq), :] for start in starts]
        carry, acc = list(state[0]), list(state[1])
        items = [(b, p) for b in range(len(starts)) for p in range(n_pairs)]
        log_sig, sp_bf16, exponent_base, w = {}, {}, {}, {}

        def logits(b, p):
            z = _dot_nt(q_pairs[p], kb[b][:, lanes_of(p)])
            log_sig = jnp.minimum(z, 0.0) - jnp.log2(1.0 + jnp.exp2(-jnp.abs(z)))
            sp = z - log_sig
            if masked:
                sp = jnp.where(valid, sp, 0.0)
            sp_bf16[b, p] = sp.astype(BF16)
            exponent_base[b, p] = (log_sig, carry[p])
            carry[p] = carry[p] + jnp.sum(sp, axis=-1, keepdims=True)

        def weights(b, p):
            after = _dot(sp_bf16[b, p], tri)
            log_sig, carried = exponent_base[b, p]
            w[b, p] = jnp.exp2(log_sig - after - carried)
            if masked:
                w[b, p] = jnp.where(valid, w[b, p], 0.0)

        def values(b, p):
            acc[p] = acc[p] + _dot(w[b, p].astype(BF16), vb[b][:, lanes_of(p)])

        stages = (logits, weights, values)
        for step in range(len(items) + len(stages) - 1):
            for depth, stage in enumerate(stages):
                if 0 <= step - depth < len(items):
                    stage(*items[step - depth])
        return carry, acc

    def start_of(block_index):
        return pl.multiple_of(block_index * tq, tq)

    init = ([jnp.zeros((2 * tq, 1), F32)] * n_pairs, [jnp.zeros((2 * tq, LANES), F32)] * n_pairs)
    state = blocks([start_of(qi)], init, True)
    state = lax.cond(qi % 2 == 1, lambda s: tuple(blocks([start_of(qi - 1)], s, False)), lambda s: s, tuple(state))
    top = qi - 1 - qi % 2

    def body(i, state):
        return tuple(blocks([start_of(top - 2 * i), start_of(top - 2 * i - 1)], state, False))

    _, acc = lax.fori_loop(0, qi // 2, body, tuple(state))
    o = jnp.concatenate([jnp.where(lane < SB_HEAD_DIM, acc[p][:tq, :], acc[p][tq:, :]) for p in range(n_pairs)],
                        axis=1)
    zg = z_ref[...].astype(F32)
    o_ref[...] = (o * (zg * _sigmoid(zg))).astype(BF16)


def _sb_attention(proj, b, t):
    tq = SB_TILE
    nq = t // tq
    return pl.pallas_call(
        _sb_kernel,
        grid=(b, nq),
        in_specs=[
            pl.BlockSpec((tq, SB_WIDTH), lambda bi, qi: (bi * nq + qi, COL_SB_Q * LANES // SB_WIDTH)),
            pl.BlockSpec((t, SB_WIDTH), lambda bi, qi: (bi, COL_SB_K * LANES // SB_WIDTH)),
            pl.BlockSpec((t, SB_WIDTH), lambda bi, qi: (bi, COL_SB_V * LANES // SB_WIDTH)),
            pl.BlockSpec((tq, SB_WIDTH), lambda bi, qi: (bi * nq + qi, COL_SB_Z * LANES // SB_WIDTH)),
        ],
        out_specs=pl.BlockSpec((tq, SB_WIDTH), lambda bi, qi: (bi * nq + qi, 0)),
        out_shape=jax.ShapeDtypeStruct((b * t, SB_WIDTH), BF16),
        compiler_params=_params("arbitrary", "arbitrary"),
        name="sb_attn",
    )(proj, proj, proj, proj)


def _ret_kernel(q_ref, k_ref, v_ref, z_ref, cos_ref, sin_ref, o_ref, qr_ref, kr_ref, *, t):
    c = RET_CHUNK
    n_chunks = t // c
    lane = lax.broadcasted_iota(jnp.int32, (1, LANES), 1)
    first_half = (lane % RET_QK_DIM) < (RET_QK_DIM // 2)
    row = lax.broadcasted_iota(jnp.int32, (c, c), 0)
    col = lax.broadcasted_iota(jnp.int32, (c, c), 1)
    diff = (row - col).astype(F32)
    pos = lax.broadcasted_iota(jnp.int32, (c, LANES), 0).astype(F32)

    def rotate(x):
        half = RET_QK_DIM // 2
        other = jnp.where(first_half, pltpu.roll(x, LANES - half, 1), pltpu.roll(x, half, 1))
        return x * cos_ref[...] + other * sin_ref[...]

    for pair in range(RET_HEADS // 2):
        cols = slice(pair * LANES, (pair + 1) * LANES)
        qr_ref[pair] = rotate(q_ref[:, cols].astype(F32))
        kr_ref[pair] = rotate(k_ref[:, cols].astype(F32))

    heads = list(range(RET_HEADS))
    log_gamma = [math.log(1.0 - 2.0 ** (-5.0 - h)) for h in heads]
    head_lanes = [(lane >= RET_QK_DIM) if h % 2 else (lane < RET_QK_DIM) for h in heads]
    intra_decay = [jnp.where(diff >= 0.0, jnp.exp(lg * jnp.maximum(diff, 0.0)), 0.0) for lg in log_gamma]
    q_decay = [jnp.exp(lg * (pos + 1.0)) for lg in log_gamma]
    k_decay = [jnp.exp(lg * (c - 1.0 - pos)) for lg in log_gamma]
    chunk_decay = [math.exp(lg * c) for lg in log_gamma]
    vcols = [slice(h * LANES, (h + 1) * LANES) for h in heads]

    def each(fn, *lists):
        return [fn(*args) for args in zip(*lists)]

    def chunk(i, states):
        rows = pl.ds(pl.multiple_of(i * c, c), c)
        qc = each(lambda h, m: jnp.where(m, qr_ref[h // 2, rows, :], 0.0), heads, head_lanes)
        kc = each(lambda h, m: jnp.where(m, kr_ref[h // 2, rows, :], 0.0), heads, head_lanes)
        vc = each(lambda cols: v_ref[rows, cols], vcols)
        scores = each(lambda q, k, d: (_dot_nt(q.astype(BF16), k.astype(BF16)) * d).astype(BF16), qc, kc, intra_decay)
        inter = each(lambda q, d, s: _dot((q * d).astype(BF16), s.astype(BF16)), qc, q_decay, states)
        states = each(lambda s, cd, k, d, v: s * cd + _dot_tn((k * d).astype(BF16), v),
                      states, chunk_decay, kc, k_decay, vc)
        o = each(lambda sc, v, x: _dot(sc, v) + x, scores, vc, inter)
        for cols, x in zip(vcols, o):
            xc = x - jnp.mean(x, axis=-1, keepdims=True)
            on = xc * lax.rsqrt(jnp.mean(xc * xc, axis=-1, keepdims=True) + EPS)
            zg = z_ref[rows, cols].astype(F32)
            o_ref[rows, cols] = (on * (zg * _sigmoid(zg))).astype(BF16)
        return states

    lax.fori_loop(0, n_chunks, chunk, [jnp.zeros((LANES, LANES), F32)] * RET_HEADS)


def _retention(proj, cos_t, sin_t, b, t):
    return pl.pallas_call(
        functools.partial(_ret_kernel, t=t),
        grid=(b,),
        in_specs=[
            pl.BlockSpec((t, RET_QK_WIDTH), lambda bi: (bi, COL_RET_Q * LANES // RET_QK_WIDTH)),
            pl.BlockSpec((t, RET_QK_WIDTH), lambda bi: (bi, COL_RET_K * LANES // RET_QK_WIDTH)),
            pl.BlockSpec((t, RET_WIDTH), lambda bi: (bi, COL_RET_V * LANES // RET_WIDTH)),
            pl.BlockSpec((t, RET_WIDTH), lambda bi: (bi, COL_RET_Z * LANES // RET_WIDTH)),
            pl.BlockSpec((t, LANES), lambda bi: (0, 0)),
            pl.BlockSpec((t, LANES), lambda bi: (0, 0)),
        ],
        out_specs=pl.BlockSpec((t, RET_WIDTH), lambda bi: (bi, 0)),
        out_shape=jax.ShapeDtypeStruct((b * t, RET_WIDTH), BF16),
        scratch_shapes=[pltpu.VMEM((RET_HEADS // 2, t, LANES), F32) for _ in range(2)],
        compiler_params=_params("arbitrary"),
        name="retention",
    )(proj, proj, proj, proj, cos_t, sin_t)


def _rope_tables(t):
    half = RET_QK_DIM // 2
    inv_freq = ROPE_BASE ** (-jnp.arange(half, dtype=F32) / half)
    ang = jnp.arange(t).astype(F32)[:, None] * inv_freq[None, :]
    cos, sin = jnp.cos(ang), jnp.sin(ang)
    return jnp.tile(cos, (1, 4)), jnp.tile(jnp.concatenate([-sin, sin], axis=-1), (1, 2))


def _gdn_kernel(q_ref, k_ref, v_ref, z_ref, ba_ref, cwq_ref, cwk_ref, cwv_ref, alog_ref, dtb_ref, gn_ref,
                o_ref, qs_ref, ks_ref, vs_ref, beta_ref, g_ref, xpad_ref, u_ref, elast_ref, w_ref, sc_ref, qd_ref,
                kd_ref, *, t):
    c = GDN_CHUNK
    lane = lax.broadcasted_iota(jnp.int32, (1, LANES), 1)
    lanes_of = lambda h: slice(h * LANES, (h + 1) * LANES)

    def conv_silu(x_ref, w_ref, h):
        x = x_ref[:, lanes_of(h)].astype(F32)
        w = w_ref[:, lanes_of(h)]
        xpad_ref[pl.ds(0, SUBLANES), :] = jnp.zeros((SUBLANES, LANES), F32)
        xpad_ref[pl.ds(SUBLANES, t), :] = x
        y = x * w[GDN_CONV - 1:GDN_CONV, :]
        for s in range(1, GDN_CONV):
            y = y + xpad_ref[pl.ds(SUBLANES - s, t), :] * w[GDN_CONV - 1 - s:GDN_CONV - s, :]
        return y * _sigmoid(y)

    def l2norm(x, scale=1.0):
        return x * (lax.rsqrt(jnp.sum(x * x, axis=-1, keepdims=True) + EPS) * scale)

    row = lax.broadcasted_iota(jnp.int32, (c, c), 0)
    col = lax.broadcasted_iota(jnp.int32, (c, c), 1)
    incl = row >= col
    strict = row > col
    tril = jnp.where(incl, 1.0, 0.0).astype(BF16)
    base_blocks = (row // GDN_INV_BASE) == (col // GDN_INV_BASE)
    sibling_blocks = []
    size = GDN_INV_BASE
    while size < c:
        sibling_blocks.append(((row // size) % 2 == 1) & ((col // size) == (row // size) - 1))
        size *= 2
    gain = gn_ref[...]
    prep_in = (qs_ref, ks_ref, vs_ref, beta_ref, g_ref)

    def each(fn, *lists):
        return [fn(*args) for args in zip(*lists)]

    def prepare_group(i, head):
        first = i * GDN_PREP_GROUP
        rows = [pl.ds(pl.multiple_of((first + j) * c, c), c) for j in range(GDN_PREP_GROUP)]
        qc, kc, vc, bb, g = ([ref[r, :] for r in rows] for ref in prep_in)
        g_split = each(_split_bf16, g)
        gc = each(lambda s: _dot(tril, s[0]) + _dot(tril, s[1]), g_split)
        decay = each(lambda x: jnp.where(incl, jnp.exp(jnp.where(incl, x - x.T, 0.0)), 0.0), gc)
        kb = each(lambda k, b_: k * b_, kc, bb)
        kcb = each(lambda k: k.astype(BF16), kc)
        neg_l = each(lambda k_b, k_c, d: -jnp.where(strict, _dot_nt(k_b.astype(BF16), k_c) * d, 0.0), kb, kcb, decay)
        power = each(lambda n: jnp.where(base_blocks, n, 0.0), neg_l)
        corr = power
        for _ in range(int(math.log2(GDN_INV_BASE)) - 1):
            power = each(lambda p: _dot_inv(p, p), power)
            corr = each(lambda m, p: m + p + _dot_inv(m, p), corr, power)
        for sib in sibling_blocks:
            c_off = each(lambda n: jnp.where(sib, n, 0.0), neg_l)
            left = each(lambda m, x: x + _dot_inv(m, x), corr, c_off)
            corr = each(lambda m, x: m + x + _dot_inv(x, m), corr, left)
        eg = each(jnp.exp, gc)
        vb = each(lambda v, b_: v * b_, vc, bb)
        kbe = each(lambda k_b, e: k_b * e, kb, eg)
        outs = (
            (u_ref, each(lambda m, x: x + _dot_inv(m, x), corr, vb)),
            (w_ref, each(lambda m, x: (x + _dot_inv(m, x)).astype(BF16), corr, kbe)),
            (sc_ref, each(lambda q, k_c, d: (_dot_nt(q.astype(BF16), k_c) * d).astype(BF16), qc, kcb, decay)),
            (qd_ref, each(lambda q, e: (q * e).astype(BF16), qc, eg)),
            (kd_ref, each(lambda k, x: (k * jnp.exp(x[c - 1:c, :] - x)).astype(BF16), kc, gc)),
        )
        for ref, vals in outs:
            for r, val in zip(rows, vals):
                ref[head, r, :] = val
        for j, x in enumerate(gc):
            elast_ref[head, pl.ds(pl.multiple_of((first + j) * SUBLANES, SUBLANES), SUBLANES), :] = (
                jnp.broadcast_to(jnp.exp(x[c - 1:c, :]), (SUBLANES, LANES)))
        return head

    ba = ba_ref[...]
    beta_all = _sigmoid(ba)
    g_all = -jnp.exp(alog_ref[...]) * _softplus(ba + dtb_ref[...])
    for head in range(GDN_HEADS):
        qs_ref[...] = l2norm(conv_silu(q_ref, cwq_ref, head), GDN_HEAD_DIM ** -0.5)
        ks_ref[...] = l2norm(conv_silu(k_ref, cwk_ref, head))
        vs_ref[...] = conv_silu(v_ref, cwv_ref, head)
        beta_ref[...] = jnp.broadcast_to(
            jnp.sum(jnp.where(lane == head, beta_all, 0.0), axis=-1, keepdims=True), (t, LANES))
        g_ref[...] = jnp.broadcast_to(
            jnp.sum(jnp.where(lane == head + GDN_HEADS, g_all, 0.0), axis=-1, keepdims=True), (t, LANES))
        lax.fori_loop(0, t // (c * GDN_PREP_GROUP), prepare_group, head)

    heads = list(range(GDN_HEADS))

    def scan(i, states):
        rows = pl.ds(pl.multiple_of(i * c, c), c)
        last_rows = pl.ds(pl.multiple_of(i * SUBLANES, SUBLANES), SUBLANES)
        sb = each(lambda s: s.astype(BF16), states)
        vnb = each(lambda h, s: (u_ref[h, rows, :] - _dot(w_ref[h, rows, :], s)).astype(BF16), heads, sb)
        o = each(lambda h, s, v: _dot(qd_ref[h, rows, :], s) + _dot(sc_ref[h, rows, :], v), heads, sb, vnb)
        states = each(lambda h, s, v: s * elast_ref[h, last_rows, :][:1, :] + _dot_tn(kd_ref[h, rows, :], v),
                      heads, states, vnb)
        for h, x in zip(heads, o):
            on = x * lax.rsqrt(jnp.mean(x * x, axis=-1, keepdims=True) + EPS) * gain
            zg = z_ref[rows, lanes_of(h)].astype(F32)
            o_ref[rows, lanes_of(h)] = (on * (zg * _sigmoid(zg))).astype(BF16)
        return states

    lax.fori_loop(0, t // c, scan, [jnp.zeros((LANES, LANES), F32)] * GDN_HEADS)


def _gdn(proj, ba, conv_w, alog_row, dtb_row, gn_row, b, t):
    wide = lambda base: pl.BlockSpec((t, GDN_WIDTH), lambda bi: (bi, base * LANES // GDN_WIDTH))
    conv_spec = lambda part: pl.BlockSpec((GDN_CONV, GDN_WIDTH), lambda bi: (0, part))
    row_spec = pl.BlockSpec((1, LANES), lambda bi: (0, 0))
    per_head = lambda dtype: pltpu.VMEM((GDN_HEADS, t, LANES), dtype)
    return pl.pallas_call(
        functools.partial(_gdn_kernel, t=t),
        grid=(b,),
        in_specs=[
            wide(COL_GDN_Q), wide(COL_GDN_K), wide(COL_GDN_V), wide(COL_GDN_Z),
            pl.BlockSpec((t, LANES), lambda bi: (bi, 0)),
            conv_spec(0), conv_spec(1), conv_spec(2),
            row_spec, row_spec, row_spec,
        ],
        out_specs=pl.BlockSpec((t, GDN_WIDTH), lambda bi: (bi, 0)),
        out_shape=jax.ShapeDtypeStruct((b * t, GDN_WIDTH), BF16),
        scratch_shapes=([pltpu.VMEM((t, LANES), F32) for _ in range(5)]
                        + [pltpu.VMEM((t + SUBLANES, LANES), F32)]
                        + [per_head(F32),pltpu.VMEM((GDN_HEADS, t // GDN_CHUNK * SUBLANES, LANES), F32)]
                        + [per_head(BF16) for _ in range(4)]),
        compiler_params=_params("arbitrary"),
        name="gdn",
    )(proj, proj, proj, proj, ba, conv_w, conv_w, conv_w, alog_row, dtb_row, gn_row)


def _merge_kernel(x_ref, g0_ref, g1_ref, g2_ref, ysb_ref, yret_ref, ygdn_ref, wsb_ref, wret_ref, wgdn_ref,
                  wout_ref, fg_ref, o_ref, *, final):
    def branch(g_ref, y_ref, w_ref):
        return _sigmoid(g_ref[...].astype(F32)) * _dot(y_ref[...], w_ref[...])

    merged = (branch(g0_ref, ysb_ref, wsb_ref) + branch(g1_ref, yret_ref, wret_ref)
              + branch(g2_ref, ygdn_ref, wgdn_ref))
    x = x_ref[...] + _dot(merged.astype(BF16), wout_ref[...])
    if final:
        x = x * lax.rsqrt(jnp.mean(x * x, axis=-1, keepdims=True) + EPS) * fg_ref[...]
    o_ref[...] = x


def _merge_out(x2, proj, ysb, yret, ygdn, wsb, wret, wgdn, wout, final_g, layer, final):
    m = x2.shape[0]
    tm = min(512, m)
    tile = lambda width, j=0: pl.BlockSpec((tm, width), lambda i: (i, j))
    of_layer = lambda a: pl.BlockSpec((None,) + a.shape[1:], lambda i: (layer, 0, 0))
    return pl.pallas_call(
        functools.partial(_merge_kernel, final=final),
        grid=(m // tm,),
        in_specs=[
            tile(D_MODEL), tile(D_MODEL, 0), tile(D_MODEL, 1), tile(D_MODEL, 2),
            tile(SB_WIDTH), tile(RET_WIDTH), tile(GDN_WIDTH),
            of_layer(wsb), of_layer(wret), of_layer(wgdn), of_layer(wout),
            pl.BlockSpec(final_g.shape, lambda i: (0, 0)),
        ],
        out_specs=tile(D_MODEL),
        out_shape=jax.ShapeDtypeStruct((m, D_MODEL), F32),
        compiler_params=_params("arbitrary"),
        name="merge_out",
    )(x2, proj, proj, proj, ysb, yret, ygdn, wsb, wret, wgdn, wout, final_g)


def _prep_in_weights(w_in):
    w_t = jnp.swapaxes(w_in, 1, 2)
    w_main = jnp.concatenate([w_t[:, _REF_GATE:, :], w_t[:, :_REF_BA, :]], axis=1).astype(BF16)
    w_ba = w_t[:, _REF_BA:_REF_BA + 2 * GDN_HEADS, :]
    w_ba = jnp.pad(w_ba, ((0, 0), (0, LANES - 2 * GDN_HEADS), (0, 0))).astype(BF16)
    return w_main, w_ba


def kernel(x, norm_g, w_in, conv_w, a_log, dt_bias, gdn_norm_g, w_sb, w_ret, w_gdn, w_out, final_g):
    b, t, d = x.shape
    depth = w_in.shape[0]
    assert d == D_MODEL and t % SB_TILE == 0 and t % RET_CHUNK == 0 and t % (GDN_CHUNK * GDN_PREP_GROUP) == 0
    w_main, w_ba = _prep_in_weights(w_in)
    scales = _proj_scales()
    w_sb, w_ret, w_gdn, w_out = (w.astype(BF16) for w in (w_sb, w_ret, w_gdn, w_out))
    pad_heads = lambda v: jnp.pad(v, ((0, 0), (GDN_HEADS, LANES - 2 * GDN_HEADS)))[:, None, :]
    alog_rows, dtb_rows = pad_heads(a_log.astype(F32)), pad_heads(dt_bias.astype(F32))
    cos_t, sin_t = _rope_tables(t)
    final_row = final_g.astype(F32)[None, :]

    x2 = x.reshape(b * t, d)
    for layer in range(depth):
        proj, ba = _in_proj(x2, norm_g[layer][None, :], w_main, w_ba, scales, layer)
        ysb = _sb_attention(proj, b, t)
        yret = _retention(proj, cos_t, sin_t, b, t)
        ygdn = _gdn(proj, ba, conv_w[layer], alog_rows[layer], dtb_rows[layer], gdn_norm_g[layer][None, :], b, t)
        x2 = _merge_out(x2, proj, ysb, yret, ygdn, w_sb, w_ret, w_gdn, w_out, final_row, layer,
                        final=(layer == depth - 1))
    return x2.reshape(b, t, d)
```

```python
import functools
import math

import jax
import jax.numpy as jnp
import numpy as np
from jax import lax
from jax.experimental import pallas as pl
from jax.experimental.pallas import tpu as pltpu

F32 = jnp.float32
BF16 = jnp.bfloat16

D_MODEL = 1024
EPS = 1e-6
LOG2E = math.log2(math.e)
LANES = 128
SUBLANES = 8
VMEM_LIMIT = 56 * 1024 * 1024

SB_HEAD_DIM = 64
SB_WIDTH = 512
SB_TILE = 256
RET_QK_DIM = 64
RET_HEADS = 4
RET_QK_WIDTH = 256
RET_WIDTH = 512
RET_CHUNK = 256
ROPE_BASE = 10000.0
GDN_HEADS = 4
GDN_HEAD_DIM = 128
GDN_WIDTH = 512
GDN_CONV = 4
GDN_CHUNK = 128
GDN_INV_BASE = 16
GDN_PREP_GROUP = 16
N_BRANCH = 3

COL_GATE = 0
COL_SB_Q, COL_SB_K, COL_SB_V, COL_SB_Z = 24, 28, 32, 36
COL_RET_Q, COL_RET_K, COL_RET_V, COL_RET_Z = 40, 42, 44, 48
COL_GDN_Q, COL_GDN_K, COL_GDN_V, COL_GDN_Z = 52, 56, 60, 64
PROJ_WIDTH = 68 * LANES
PROJ_TILE_N = 512
PROJ_GATE_TILES = N_BRANCH * D_MODEL // PROJ_TILE_N
_REF_BA = 5632
_REF_GATE = 5640


def _dot(a, b, contract=((1,), (0,)), precision=None):
    return lax.dot_general(a, b, (contract, ((), ())), precision=precision,
                           preferred_element_type=F32)


def _dot_nt(a, b, precision=None):
    return _dot(a, b, ((1,), (1,)), precision)


def _dot_tn(a, b, precision=None):
    return _dot(a, b, ((0,), (0,)), precision)


def _split_bf16(a):
    hi = a.astype(BF16)
    return hi, (a - hi.astype(F32)).astype(BF16)


def _dot_inv(a, b):
    return _dot(a.astype(BF16), b.astype(BF16))


def _sigmoid(x):
    return 1.0 / (1.0 + jnp.exp2(x * -LOG2E))


def _softplus(x):
    return jnp.maximum(x, 0.0) + jnp.log(1.0 + jnp.exp(-jnp.abs(x)))


def _params(*sem):
    return pltpu.CompilerParams(dimension_semantics=sem, vmem_limit_bytes=VMEM_LIMIT)


def _in_proj_kernel(x_ref, g_ref, wgate_ref, wrest_ref, wba_ref, scale_ref, proj_ref, ba_ref, hn_ref):
    j = pl.program_id(1)

    @pl.when(j == 0)
    def _():
        x = x_ref[...]
        ms = jnp.mean(x * x, axis=-1, keepdims=True)
        hn = (x * lax.rsqrt(ms + EPS) * g_ref[...]).astype(BF16)
        hn_ref[...] = hn
        ba_ref[...] = _dot_nt(hn, wba_ref[...])

    w = jnp.where(j < PROJ_GATE_TILES, wgate_ref[...], wrest_ref[...])
    proj_ref[...] = (_dot_nt(hn_ref[...], w) * scale_ref[:1, :]).astype(BF16)


def _proj_scales():
    scales = np.ones((PROJ_WIDTH,), np.float32)
    scales[COL_SB_Q * LANES:COL_SB_Q * LANES + SB_WIDTH] = SB_HEAD_DIM ** -0.5 * LOG2E
    scales[COL_RET_K * LANES:COL_RET_K * LANES + RET_QK_WIDTH] = RET_QK_DIM ** -0.5
    return jnp.asarray(np.repeat(scales.reshape(-1, 1, PROJ_TILE_N), SUBLANES, axis=1).reshape(-1, PROJ_TILE_N))


def _in_proj(x2, gain, w_gate, w_all, wba_all, scales, layer):
    m = x2.shape[0]
    tm = min(2048, m)
    weight_tile = lambda index: pl.BlockSpec((None, PROJ_TILE_N, D_MODEL), lambda i, j: (layer, index(j), 0))
    return pl.pallas_call(
        _in_proj_kernel,
        grid=(m // tm, PROJ_WIDTH // PROJ_TILE_N),
        in_specs=[
            pl.BlockSpec((tm, D_MODEL), lambda i, j: (i, 0)),
            pl.BlockSpec((1, D_MODEL), lambda i, j: (0, 0)),
            weight_tile(lambda j: jnp.minimum(j, PROJ_GATE_TILES - 1)),
            weight_tile(lambda j: jnp.maximum(j - PROJ_GATE_TILES, 0)),
            pl.BlockSpec((None, LANES, D_MODEL), lambda i, j: (layer, 0, 0)),
            pl.BlockSpec((SUBLANES, PROJ_TILE_N), lambda i, j: (j, 0)),
        ],
        out_specs=[
            pl.BlockSpec((tm, PROJ_TILE_N), lambda i, j: (i, j)),
            pl.BlockSpec((tm, LANES), lambda i, j: (i, 0)),
        ],
        out_shape=[
            jax.ShapeDtypeStruct((m, PROJ_WIDTH), BF16),
            jax.ShapeDtypeStruct((m, LANES), F32),
        ],
        scratch_shapes=[pltpu.VMEM((tm, D_MODEL), BF16)],
        compiler_params=_params("arbitrary", "arbitrary"),
        name="in_proj",
    )(x2, gain, w_gate, w_all, wba_all, scales)


def _sb_kernel(q_ref, k_ref, v_ref, z_ref, o_ref):
    tq = SB_TILE
    n_pairs = SB_WIDTH // LANES
    qi = pl.program_id(1)
    lane = lax.broadcasted_iota(jnp.int32, (1, LANES), 1)
    row = lax.broadcasted_iota(jnp.int32, (tq, tq), 0)
    col = lax.broadcasted_iota(jnp.int32, (tq, tq), 1)
    tri = jnp.where(row > col, 1.0, 0.0).astype(BF16)
    row2 = lax.broadcasted_iota(jnp.int32, (2 * tq, tq), 0) % tq
    col2 = lax.broadcasted_iota(jnp.int32, (2 * tq, tq), 1)
    valid = col2 < row2
    lanes_of = lambda p: slice(p * LANES, (p + 1) * LANES)

    q = q_ref[...]
    zero = jnp.zeros((tq, LANES), BF16)
    q_pairs = []
    for p in range(n_pairs):
        qp = q[:, lanes_of(p)]
        q_pairs.append(jnp.concatenate(
            [jnp.where(lane < SB_HEAD_DIM, qp, zero), jnp.where(lane >= SB_HEAD_DIM, qp, zero)], axis=0))

    def blocks(starts, state, masked):
        kb = [k_ref[pl.ds(start, tq), :] for start in starts]
        vb = [v_ref[pl.ds(start, tq), :] for start in starts]
        carry, acc = list(state[0]), list(state[1])
        items = [(b, p) for b in range(len(starts)) for p in range(n_pairs)]
        log_sig, sp_bf16, exponent_base, w = {}, {}, {}, {}

        def logits(b, p):
            z = _dot_nt(q_pairs[p], kb[b][:, lanes_of(p)])
            log_sig = jnp.minimum(z, 0.0) - jnp.log2(1.0 + jnp.exp2(-jnp.abs(z)))
            sp = z - log_sig
            if masked:
                sp = jnp.where(valid, sp, 0.0)
            sp_bf16[b, p] = sp.astype(BF16)
            exponent_base[b, p] = (log_sig, carry[p])
            carry[p] = carry[p] + jnp.sum(sp, axis=-1, keepdims=True)

        def weights(b, p):
            after = _dot(sp_bf16[b, p], tri)
            log_sig, carried = exponent_base[b, p]
            w[b, p] = jnp.exp2(log_sig - after - carried)
            if masked:
                w[b, p] = jnp.where(valid, w[b, p], 0.0)

        def values(b, p):
            acc[p] = acc[p] + _dot(w[b, p].astype(BF16), vb[b][:, lanes_of(p)])

        stages = (logits, weights, values)
        for step in range(len(items) + len(stages) - 1):
            for depth, stage in enumerate(stages):
                if 0 <= step - depth < len(items):
                    stage(*items[step - depth])
        return carry, acc

    def start_of(block_index):
        return pl.multiple_of(block_index * tq, tq)

    init = ([jnp.zeros((2 * tq, 1), F32)] * n_pairs, [jnp.zeros((2 * tq, LANES), F32)] * n_pairs)
    state = blocks([start_of(qi)], init, True)
    state = lax.cond(qi % 2 == 1, lambda s: tuple(blocks([start_of(qi - 1)], s, False)), lambda s: s, tuple(state))
    top = qi - 1 - qi % 2

    def body(i, state):
        return tuple(blocks([start_of(top - 2 * i), start_of(top - 2 * i - 1)], state, False))

    _, acc = lax.fori_loop(0, qi // 2, body, tuple(state))
    o = jnp.concatenate([jnp.where(lane < SB_HEAD_DIM, acc[p][:tq, :], acc[p][tq:, :]) for p in range(n_pairs)],
                        axis=1)
    zg = z_ref[...].astype(F32)
    o_ref[...] = (o * (zg * _sigmoid(zg))).astype(BF16)


def _sb_attention(proj, b, t):
    tq = SB_TILE
    nq = t // tq
    return pl.pallas_call(
        _sb_kernel,
        grid=(b, nq),
        in_specs=[
            pl.BlockSpec((tq, SB_WIDTH), lambda bi, qi: (bi * nq + qi, COL_SB_Q * LANES // SB_WIDTH)),
            pl.BlockSpec((t, SB_WIDTH), lambda bi, qi: (bi, COL_SB_K * LANES // SB_WIDTH)),
            pl.BlockSpec((t, SB_WIDTH), lambda bi, qi: (bi, COL_SB_V * LANES // SB_WIDTH)),
            pl.BlockSpec((tq, SB_WIDTH), lambda bi, qi: (bi * nq + qi, COL_SB_Z * LANES // SB_WIDTH)),
        ],
        out_specs=pl.BlockSpec((tq, SB_WIDTH), lambda bi, qi: (bi * nq + qi, 0)),
        out_shape=jax.ShapeDtypeStruct((b * t, SB_WIDTH), BF16),
        compiler_params=_params("arbitrary", "arbitrary"),
        name="sb_attn",
    )(proj, proj, proj, proj)


def _ret_kernel(q_ref, k_ref, v_ref, z_ref, cos_ref, sin_ref, o_ref, qr_ref, kr_ref, *, t):
    c = RET_CHUNK
    n_chunks = t // c
    lane = lax.broadcasted_iota(jnp.int32, (1, LANES), 1)
    first_half = (lane % RET_QK_DIM) < (RET_QK_DIM // 2)
    row = lax.broadcasted_iota(jnp.int32, (c, c), 0)
    col = lax.broadcasted_iota(jnp.int32, (c, c), 1)
    diff = (row - col).astype(F32)
    pos = lax.broadcasted_iota(jnp.int32, (c, LANES), 0).astype(F32)

    def rotate(x):
        half = RET_QK_DIM // 2
        other = jnp.where(first_half, pltpu.roll(x, LANES - half, 1), pltpu.roll(x, half, 1))
        return x * cos_ref[...] + other * sin_ref[...]

    for pair in range(RET_HEADS // 2):
        cols = slice(pair * LANES, (pair + 1) * LANES)
        qr_ref[pair] = rotate(q_ref[:, cols].astype(F32))
        kr_ref[pair] = rotate(k_ref[:, cols].astype(F32))

    heads = list(range(RET_HEADS))
    log_gamma = [math.log(1.0 - 2.0 ** (-5.0 - h)) for h in heads]
    head_lanes = [(lane >= RET_QK_DIM) if h % 2 else (lane < RET_QK_DIM) for h in heads]
    intra_decay = [jnp.where(diff >= 0.0, jnp.exp(lg * jnp.maximum(diff, 0.0)), 0.0) for lg in log_gamma]
    q_decay = [jnp.exp(lg * (pos + 1.0)) for lg in log_gamma]
    k_decay = [jnp.exp(lg * (c - 1.0 - pos)) for lg in log_gamma]
    chunk_decay = [math.exp(lg * c) for lg in log_gamma]
    vcols = [slice(h * LANES, (h + 1) * LANES) for h in heads]

    def each(fn, *lists):
        return [fn(*args) for args in zip(*lists)]

    def chunk(i, states):
        rows = pl.ds(pl.multiple_of(i * c, c), c)
        qc = each(lambda h, m: jnp.where(m, qr_ref[h // 2, rows, :], 0.0), heads, head_lanes)
        kc = each(lambda h, m: jnp.where(m, kr_ref[h // 2, rows, :], 0.0), heads, head_lanes)
        vc = each(lambda cols: v_ref[rows, cols], vcols)
        scores = each(lambda q, k, d: (_dot_nt(q.astype(BF16), k.astype(BF16)) * d).astype(BF16), qc, kc, intra_decay)
        inter = each(lambda q, d, s: _dot((q * d).astype(BF16), s.astype(BF16)), qc, q_decay, states)
        states = each(lambda s, cd, k, d, v: s * cd + _dot_tn((k * d).astype(BF16), v),
                      states, chunk_decay, kc, k_decay, vc)
        o = each(lambda sc, v, x: _dot(sc, v) + x, scores, vc, inter)
        for cols, x in zip(vcols, o):
            xc = x - jnp.mean(x, axis=-1, keepdims=True)
            on = xc * lax.rsqrt(jnp.mean(xc * xc, axis=-1, keepdims=True) + EPS)
            zg = z_ref[rows, cols].astype(F32)
            o_ref[rows, cols] = (on * (zg * _sigmoid(zg))).astype(BF16)
        return states

    lax.fori_loop(0, n_chunks, chunk, [jnp.zeros((LANES, LANES), F32)] * RET_HEADS)


def _retention(proj, cos_t, sin_t, b, t):
    return pl.pallas_call(
        functools.partial(_ret_kernel, t=t),
        grid=(b,),
        in_specs=[
            pl.BlockSpec((t, RET_QK_WIDTH), lambda bi: (bi, COL_RET_Q * LANES // RET_QK_WIDTH)),
            pl.BlockSpec((t, RET_QK_WIDTH), lambda bi: (bi, COL_RET_K * LANES // RET_QK_WIDTH)),
            pl.BlockSpec((t, RET_WIDTH), lambda bi: (bi, COL_RET_V * LANES // RET_WIDTH)),
            pl.BlockSpec((t, RET_WIDTH), lambda bi: (bi, COL_RET_Z * LANES // RET_WIDTH)),
            pl.BlockSpec((t, LANES), lambda bi: (0, 0)),
            pl.BlockSpec((t, LANES), lambda bi: (0, 0)),
        ],
        out_specs=pl.BlockSpec((t, RET_WIDTH), lambda bi: (bi, 0)),
        out_shape=jax.ShapeDtypeStruct((b * t, RET_WIDTH), BF16),
        scratch_shapes=[pltpu.VMEM((RET_HEADS // 2, t, LANES), F32) for _ in range(2)],
        compiler_params=_params("arbitrary"),
        name="retention",
    )(proj, proj, proj, proj, cos_t, sin_t)


def _rope_tables(t):
    half = RET_QK_DIM // 2
    inv_freq = ROPE_BASE ** (-jnp.arange(half, dtype=F32) / half)
    ang = jnp.arange(t).astype(F32)[:, None] * inv_freq[None, :]
    cos, sin = jnp.cos(ang), jnp.sin(ang)
    return jnp.tile(cos, (1, 4)), jnp.tile(jnp.concatenate([-sin, sin], axis=-1), (1, 2))


def _gdn_kernel(q_ref, k_ref, v_ref, z_ref, ba_ref, cwq_ref, cwk_ref, cwv_ref, alog_ref, dtb_ref, gn_ref,
                o_ref, qs_ref, ks_ref, vs_ref, beta_ref, g_ref, xpad_ref, u_ref, elast_ref, w_ref, sc_ref, qd_ref,
                kd_ref, *, t):
    c = GDN_CHUNK
    lane = lax.broadcasted_iota(jnp.int32, (1, LANES), 1)
    lanes_of = lambda h: slice(h * LANES, (h + 1) * LANES)

    def conv_silu(x_ref, w_ref, h):
        x = x_ref[:, lanes_of(h)].astype(F32)
        w = w_ref[:, lanes_of(h)]
        xpad_ref[pl.ds(0, SUBLANES), :] = jnp.zeros((SUBLANES, LANES), F32)
        xpad_ref[pl.ds(SUBLANES, t), :] = x
        y = x * w[GDN_CONV - 1:GDN_CONV, :]
        for s in range(1, GDN_CONV):
            y = y + xpad_ref[pl.ds(SUBLANES - s, t), :] * w[GDN_CONV - 1 - s:GDN_CONV - s, :]
        return y * _sigmoid(y)

    def l2norm(x, scale=1.0):
        return x * (lax.rsqrt(jnp.sum(x * x, axis=-1, keepdims=True) + EPS) * scale)

    row = lax.broadcasted_iota(jnp.int32, (c, c), 0)
    col = lax.broadcasted_iota(jnp.int32, (c, c), 1)
    incl = row >= col
    strict = row > col
    tril = jnp.where(incl, 1.0, 0.0).astype(BF16)
    base_blocks = (row // GDN_INV_BASE) == (col // GDN_INV_BASE)
    sibling_blocks = []
    size = GDN_INV_BASE
    while size < c:
        sibling_blocks.append(((row // size) % 2 == 1) & ((col // size) == (row // size) - 1))
        size *= 2
    gain = gn_ref[...]
    prep_in = (qs_ref, ks_ref, vs_ref, beta_ref, g_ref)

    def each(fn, *lists):
        return [fn(*args) for args in zip(*lists)]

    def prepare_group(i, head):
        first = i * GDN_PREP_GROUP
        rows = [pl.ds(pl.multiple_of((first + j) * c, c), c) for j in range(GDN_PREP_GROUP)]
        qc, kc, vc, bb, g = ([ref[r, :] for r in rows] for ref in prep_in)
        g_split = each(_split_bf16, g)
        gc = each(lambda s: _dot(tril, s[0]) + _dot(tril, s[1]), g_split)
        decay = each(lambda x: jnp.where(incl, jnp.exp(jnp.where(incl, x - x.T, 0.0)), 0.0), gc)
        kb = each(lambda k, b_: k * b_, kc, bb)
        kcb = each(lambda k: k.astype(BF16), kc)
        neg_l = each(lambda k_b, k_c, d: -jnp.where(strict, _dot_nt(k_b.astype(BF16), k_c) * d, 0.0), kb, kcb, decay)
        power = each(lambda n: jnp.where(base_blocks, n, 0.0), neg_l)
        corr = power
        for _ in range(int(math.log2(GDN_INV_BASE)) - 1):
            power = each(lambda p: _dot_inv(p, p), power)
            corr = each(lambda m, p: m + p + _dot_inv(m, p), corr, power)
        for sib in sibling_blocks:
            c_off = each(lambda n: jnp.where(sib, n, 0.0), neg_l)
            left = each(lambda m, x: x + _dot_inv(m, x), corr, c_off)
            corr = each(lambda m, x: m + x + _dot_inv(x, m), corr, left)
        eg = each(jnp.exp, gc)
        vb = each(lambda v, b_: v * b_, vc, bb)
        kbe = each(lambda k_b, e: k_b * e, kb, eg)
        outs = (
            (u_ref, each(lambda m, x: x + _dot_inv(m, x), corr, vb)),
            (w_ref, each(lambda m, x: (x + _dot_inv(m, x)).astype(BF16), corr, kbe)),
            (sc_ref, each(lambda q, k_c, d: (_dot_nt(q.astype(BF16), k_c) * d).astype(BF16), qc, kcb, decay)),
            (qd_ref, each(lambda q, e: (q * e).astype(BF16), qc, eg)),
            (kd_ref, each(lambda k, x: (k * jnp.exp(x[c - 1:c, :] - x)).astype(BF16), kc, gc)),
        )
        for ref, vals in outs:
            for r, val in zip(rows, vals):
                ref[head, r, :] = val
        for j, x in enumerate(gc):
            elast_ref[head, pl.ds(pl.multiple_of((first + j) * SUBLANES, SUBLANES), SUBLANES), :] = (
                jnp.broadcast_to(jnp.exp(x[c - 1:c, :]), (SUBLANES, LANES)))
        return head

    ba = ba_ref[...]
    beta_all = _sigmoid(ba)
    g_all = -jnp.exp(alog_ref[...]) * _softplus(ba + dtb_ref[...])
    for head in range(GDN_HEADS):
        qs_ref[...] = l2norm(conv_silu(q_ref, cwq_ref, head), GDN_HEAD_DIM ** -0.5)
        ks_ref[...] = l2norm(conv_silu(k_ref, cwk_ref, head))
        vs_ref[...] = conv_silu(v_ref, cwv_ref, head)
        beta_ref[...] = jnp.broadcast_to(
            jnp.sum(jnp.where(lane == head, beta_all, 0.0), axis=-1, keepdims=True), (t, LANES))
        g_ref[...] = jnp.broadcast_to(
            jnp.sum(jnp.where(lane == head + GDN_HEADS, g_all, 0.0), axis=-1, keepdims=True), (t, LANES))
        lax.fori_loop(0, t // (c * GDN_PREP_GROUP), prepare_group, head)

    heads = list(range(GDN_HEADS))

    def scan(i, states):
        rows = pl.ds(pl.multiple_of(i * c, c), c)
        last_rows = pl.ds(pl.multiple_of(i * SUBLANES, SUBLANES), SUBLANES)
        sb = each(lambda s: s.astype(BF16), states)
        vnb = each(lambda h, s: (u_ref[h, rows, :] - _dot(w_ref[h, rows, :], s)).astype(BF16), heads, sb)
        o = each(lambda h, s, v: _dot(qd_ref[h, rows, :], s) + _dot(sc_ref[h, rows, :], v), heads, sb, vnb)
        states = each(lambda h, s, v: s * elast_ref[h, last_rows, :][:1, :] + _dot_tn(kd_ref[h, rows, :], v),
                      heads, states, vnb)
        for h, x in zip(heads, o):
            on = x * lax.rsqrt(jnp.mean(x * x, axis=-1, keepdims=True) + EPS) * gain
            zg = z_ref[rows, lanes_of(h)].astype(F32)
            o_ref[rows, lanes_of(h)] = (on * (zg * _sigmoid(zg))).astype(BF16)
        return states

    lax.fori_loop(0, t // c, scan, [jnp.zeros((LANES, LANES), F32)] * GDN_HEADS)


def _gdn(proj, ba, conv_w, alog_row, dtb_row, gn_row, b, t):
    wide = lambda base: pl.BlockSpec((t, GDN_WIDTH), lambda bi: (bi, base * LANES // GDN_WIDTH))
    conv_spec = lambda part: pl.BlockSpec((GDN_CONV, GDN_WIDTH), lambda bi: (0, part))
    row_spec = pl.BlockSpec((1, LANES), lambda bi: (0, 0))
    per_head = lambda dtype: pltpu.VMEM((GDN_HEADS, t, LANES), dtype)
    return pl.pallas_call(
        functools.partial(_gdn_kernel, t=t),
        grid=(b,),
        in_specs=[
            wide(COL_GDN_Q), wide(COL_GDN_K), wide(COL_GDN_V), wide(COL_GDN_Z),
            pl.BlockSpec((t, LANES), lambda bi: (bi, 0)),
            conv_spec(0), conv_spec(1), conv_spec(2),
            row_spec, row_spec, row_spec,
        ],
        out_specs=pl.BlockSpec((t, GDN_WIDTH), lambda bi: (bi, 0)),
        out_shape=jax.ShapeDtypeStruct((b * t, GDN_WIDTH), BF16),
        scratch_shapes=([pltpu.VMEM((t, LANES), F32) for _ in range(5)]
                        + [pltpu.VMEM((t + SUBLANES, LANES), F32)]
                        + [per_head(F32),pltpu.VMEM((GDN_HEADS, t // GDN_CHUNK * SUBLANES, LANES), F32)]
                        + [per_head(BF16) for _ in range(4)]),
        compiler_params=_params("arbitrary"),
        name="gdn",
    )(proj, proj, proj, proj, ba, conv_w, conv_w, conv_w, alog_row, dtb_row, gn_row)


def _merge_kernel(x_ref, g0_ref, g1_ref, g2_ref, ysb_ref, yret_ref, ygdn_ref, wsb_ref, wret_ref, wgdn_ref,
                  wout_ref, fg_ref, o_ref, *, final):
    def branch(g_ref, y_ref, w_ref):
        return _sigmoid(g_ref[...].astype(F32)) * _dot(y_ref[...], w_ref[...])

    merged = (branch(g0_ref, ysb_ref, wsb_ref) + branch(g1_ref, yret_ref, wret_ref)
              + branch(g2_ref, ygdn_ref, wgdn_ref))
    x = x_ref[...] + _dot(merged.astype(BF16), wout_ref[...])
    if final:
        x = x * lax.rsqrt(jnp.mean(x * x, axis=-1, keepdims=True) + EPS) * fg_ref[...]
    o_ref[...] = x


def _merge_out(x2, proj, ysb, yret, ygdn, wsb, wret, wgdn, wout, final_g, layer, final):
    m = x2.shape[0]
    tm = min(512, m)
    tile = lambda width, j=0: pl.BlockSpec((tm, width), lambda i: (i, j))
    of_layer = lambda a: pl.BlockSpec((None,) + a.shape[1:], lambda i: (layer, 0, 0))
    return pl.pallas_call(
        functools.partial(_merge_kernel, final=final),
        grid=(m // tm,),
        in_specs=[
            tile(D_MODEL), tile(D_MODEL, 0), tile(D_MODEL, 1), tile(D_MODEL, 2),
            tile(SB_WIDTH), tile(RET_WIDTH), tile(GDN_WIDTH),
            of_layer(wsb), of_layer(wret), of_layer(wgdn), of_layer(wout),
            pl.BlockSpec(final_g.shape, lambda i: (0, 0)),
        ],
        out_specs=tile(D_MODEL),
        out_shape=jax.ShapeDtypeStruct((m, D_MODEL), F32),
        compiler_params=_params("arbitrary"),
        name="merge_out",
    )(x2, proj, proj, proj, ysb, yret, ygdn, wsb, wret, wgdn, wout, final_g)


def _prep_in_weights(w_in):
    w_all = jnp.swapaxes(w_in, 1, 2).astype(BF16)
    w_gate = w_all[:, _REF_GATE:, :]
    w_ba = w_all[:, _REF_BA:_REF_BA + 2 * GDN_HEADS, :]
    w_ba = jnp.pad(w_ba, ((0, 0), (0, LANES - 2 * GDN_HEADS), (0, 0)))
    return w_gate, w_all, w_ba


def kernel(x, norm_g, w_in, conv_w, a_log, dt_bias, gdn_norm_g, w_sb, w_ret, w_gdn, w_out, final_g):
    b, t, d = x.shape
    depth = w_in.shape[0]
    assert d == D_MODEL and t % SB_TILE == 0 and t % RET_CHUNK == 0 and t % (GDN_CHUNK * GDN_PREP_GROUP) == 0
    w_gate, w_all, w_ba = _prep_in_weights(w_in)
    scales = _proj_scales()
    w_sb, w_ret, w_gdn, w_out = (w.astype(BF16) for w in (w_sb, w_ret, w_gdn, w_out))
    pad_heads = lambda v: jnp.pad(v, ((0, 0), (GDN_HEADS, LANES - 2 * GDN_HEADS)))[:, None, :]
    alog_rows, dtb_rows = pad_heads(a_log.astype(F32)), pad_heads(dt_bias.astype(F32))
    cos_t, sin_t = _rope_tables(t)
    final_row = final_g.astype(F32)[None, :]

    x2 = x.reshape(b * t, d)
    for layer in range(depth):
        proj, ba = _in_proj(x2, norm_g[layer][None, :], w_gate, w_all, w_ba, scales, layer)
        ysb = _sb_attention(proj, b, t)
        yret = _retention(proj, cos_t, sin_t, b, t)
        ygdn = _gdn(proj, ba, conv_w[layer], alog_rows[layer], dtb_rows[layer], gdn_norm_g[layer][None, :], b, t)
        x2 = _merge_out(x2, proj, ysb, yret, ygdn, w_sb, w_ret, w_gdn, w_out, final_row, layer,
                        final=(layer == depth - 1))
    return x2.reshape(b, t, d)
```
